```python
import math
import jax, jax.numpy as jnp
from jax import lax
import numpy as np

D_MODEL = 2048
BATCH = 8
SEQ = 4096
DEPTH = 2
DEC_BATCH = 16
DEC_SEQ = 64
PAST_LEN = 4096

CHUNK = 64
Q_BLOCK = 128
HEAD_DIM = 128
H_A = 8
H_B = 4
H_C = 4
DIFF_DIM = HEAD_DIM // 2
BAND_CHUNKS = 8
WINDOW_B = BAND_CHUNKS * CHUNK
REL_MAX_PAST = 128
REL_SIZE = REL_MAX_PAST + CHUNK
MIX_WIDTH = (H_A + H_B + H_C) * HEAD_DIM
D_FF = 4 * D_MODEL
ROPE_THETA = 10000.0
EPS = 1e-6
NEG = -1e30

kernel_name = 'hybrid_streaming_fox_band_diff_encoder_step'


def _split_sizes():
    return [H_A * HEAD_DIM, H_A * HEAD_DIM, H_A * HEAD_DIM, H_A,
            H_B * HEAD_DIM, H_B * HEAD_DIM, H_B * HEAD_DIM,
            H_C * 2 * DIFF_DIM, H_C * 2 * DIFF_DIM, H_C * HEAD_DIM]


def rmsnorm(x, g):
    xf = x.astype(jnp.float32)
    y = xf * lax.rsqrt(jnp.mean(xf * xf, axis=-1, keepdims=True) + EPS)
    return (y * g.astype(jnp.float32)).astype(x.dtype)


def rope(x, pos):
    dim = x.shape[-1]
    half = dim // 2
    inv = ROPE_THETA ** (-jnp.arange(half, dtype=jnp.float32) * 2.0 / dim)
    ang = pos.astype(jnp.float32)[:, None] * inv[None, :]
    shp = (pos.shape[0],) + (1,) * (x.ndim - 3) + (half,)
    cos = jnp.cos(ang).reshape(shp)
    sin = jnp.sin(ang).reshape(shp)
    xf = x.astype(jnp.float32)
    x1, x2 = xf[..., :half], xf[..., half:]
    return jnp.concatenate([x1 * cos - x2 * sin, x2 * cos + x1 * sin], axis=-1).astype(x.dtype)


def project(h, pos, w_in, b_f, g_qa, g_ka, g_qb, g_kb, g_qc, g_kc):
    bsz, t = h.shape[0], h.shape[1]
    p = jnp.einsum('btd,dp->btp', h, w_in)
    sizes = _split_sizes()
    cuts = [sum(sizes[:i + 1]) for i in range(len(sizes) - 1)]
    qa, ka, va, fa, qb, kb, vb, qc, kc, vc = jnp.split(p, cuts, axis=-1)
    qa = rmsnorm(qa.reshape(bsz, t, H_A, HEAD_DIM), g_qa)
    ka = rmsnorm(ka.reshape(bsz, t, H_A, HEAD_DIM), g_ka)
    va = va.reshape(bsz, t, H_A, HEAD_DIM)
    loga = jax.nn.log_sigmoid((fa + b_f).astype(jnp.float32))
    qb = rmsnorm(qb.reshape(bsz, t, H_B, HEAD_DIM), g_qb)
    kb = rmsnorm(kb.reshape(bsz, t, H_B, HEAD_DIM), g_kb)
    vb = vb.reshape(bsz, t, H_B, HEAD_DIM)
    qc = rope(rmsnorm(qc.reshape(bsz, t, H_C, 2, DIFF_DIM), g_qc), pos)
    kc = rope(rmsnorm(kc.reshape(bsz, t, H_C, 2, DIFF_DIM), g_kc), pos)
    vc = vc.reshape(bsz, t, H_C, HEAD_DIM)
    return qa, ka, va, loga, qb, kb, vb, qc, kc, vc


def sweep_blocks(fn, n, block):
    out = lax.map(fn, jnp.arange(n // block) * block)
    out = jnp.moveaxis(out, 0, 1)
    return out.reshape((out.shape[0], n) + out.shape[3:])


def fox_attend(q, cq, pos_q, k, v, ck, pos_k):
    s = jnp.einsum('bqhd,bkhd->bhqk', q, k).astype(jnp.float32) * (HEAD_DIM ** -0.5)
    s = s + jnp.transpose(cq, (0, 2, 1))[:, :, :, None] - jnp.transpose(ck, (0, 2, 1))[:, :, None, :]
    s = jnp.where(pos_k[None, :] <= pos_q[:, None], s, NEG)
    p = jax.nn.softmax(s, axis=-1)
    return jnp.einsum('bhqk,bkhd->bqhd', p.astype(v.dtype), v)


def band_attend(q, pos_q, k, v, pos_k, rel_table):
    s = jnp.einsum('bqhd,bkhd->bhqk', q, k).astype(jnp.float32) * (HEAD_DIM ** -0.5)
    rel = pos_q[:, None] - pos_k[None, :]
    idx = jnp.clip(rel, -(CHUNK - 1), REL_MAX_PAST) + (CHUNK - 1)
    s = s + rel_table.astype(jnp.float32)[:, idx][None]
    cq = pos_q // CHUNK
    ck = pos_k // CHUNK
    mask = (pos_k[None, :] >= 0) & (ck[None, :] <= cq[:, None]) & (ck[None, :] >= cq[:, None] - BAND_CHUNKS)
    s = jnp.where(mask, s, NEG)
    p = jax.nn.softmax(s, axis=-1)
    return jnp.einsum('bhqk,bkhd->bqhd', p.astype(v.dtype), v)


def diff_attend(q, pos_q, k, v, pos_k, lam):
    s = jnp.einsum('bqhmd,bkhmd->bmhqk', q, k).astype(jnp.float32) * (DIFF_DIM ** -0.5)
    mask = (pos_k[None, :] // CHUNK) <= (pos_q[:, None] // CHUNK)
    s = jnp.where(mask, s, NEG)
    p = jax.nn.softmax(s, axis=-1)
    a = p[:, 0] - lam * p[:, 1]
    return jnp.einsum('bhqk,bkhd->bqhd', a.astype(v.dtype), v)


def fox_prompt(q, k, v, logf):
    n = q.shape[1]
    c = jnp.cumsum(logf, axis=1)
    pos = jnp.arange(n)

    def blk(start):
        qs = lax.dynamic_slice_in_dim(q, start, Q_BLOCK, axis=1)
        cs = lax.dynamic_slice_in_dim(c, start, Q_BLOCK, axis=1)
        return fox_attend(qs, cs, start + jnp.arange(Q_BLOCK), k, v, c, pos)

    return sweep_blocks(blk, n, Q_BLOCK)


def band_prompt(q, k, v, rel_table):
    n = q.shape[1]
    pad = ((0, 0), (WINDOW_B, 0), (0, 0), (0, 0))
    kp = jnp.pad(k, pad)
    vp = jnp.pad(v, pad)

    def blk(start):
        qs = lax.dynamic_slice_in_dim(q, start, CHUNK, axis=1)
        ks = lax.dynamic_slice_in_dim(kp, start, WINDOW_B + CHUNK, axis=1)
        vs = lax.dynamic_slice_in_dim(vp, start, WINDOW_B + CHUNK, axis=1)
        pk = start - WINDOW_B + jnp.arange(WINDOW_B + CHUNK)
        return band_attend(qs, start + jnp.arange(CHUNK), ks, vs, pk, rel_table)

    return sweep_blocks(blk, n, CHUNK)


def diff_prompt(q, k, v, lam):
    n = q.shape[1]
    pos = jnp.arange(n)

    def blk(start):
        qs = lax.dynamic_slice_in_dim(q, start, Q_BLOCK, axis=1)
        return diff_attend(qs, start + jnp.arange(Q_BLOCK), k, v, pos, lam)

    return sweep_blocks(blk, n, Q_BLOCK)


def diff_lambda(lq1, lk1, lq2, lk2, lam_init):
    f = jnp.float32
    return (jnp.exp(jnp.sum(lq1.astype(f) * lk1.astype(f))) -
            jnp.exp(jnp.sum(lq2.astype(f) * lk2.astype(f))) + lam_init)


def merge_heads(oa, ob, oc, g_subln, lam_init, w_out):
    bsz, t = oa.shape[0], oa.shape[1]
    oc = rmsnorm(oc, g_subln) * (1.0 - lam_init)
    o = jnp.concatenate([oa.reshape(bsz, t, H_A * HEAD_DIM),
                         ob.reshape(bsz, t, H_B * HEAD_DIM),
                         oc.reshape(bsz, t, H_C * HEAD_DIM)], axis=-1)
    return jnp.einsum('btm,md->btd', o, w_out)


def sqrelu_mlp(h, w_up, w_down):
    u = jnp.einsum('btd,df->btf', h, w_up)
    return jnp.einsum('btf,fd->btd', jnp.square(jax.nn.relu(u)), w_down)


def setup_inputs(seed: int = 0) -> dict:
    key = jax.random.key(seed)
    ks = jax.random.split(key, 32)
    f32 = jnp.float32

    def nrm(k, shape, s=1.0):
        return s * jax.random.normal(k, shape, f32)

    b_rows = min(WINDOW_B, PAST_LEN)
    p_width = sum(_split_sizes())
    return {
        'x_prompt': nrm(ks[0], (BATCH, SEQ, D_MODEL)),
        'x_sample': nrm(ks[1], (DEC_BATCH, DEC_SEQ, D_MODEL)),
        'cache_a_k': nrm(ks[2], (DEPTH, DEC_BATCH, PAST_LEN, H_A, HEAD_DIM)),
        'cache_a_v': nrm(ks[3], (DEPTH, DEC_BATCH, PAST_LEN, H_A, HEAD_DIM)),
        'cache_a_logf': jax.nn.log_sigmoid(2.0 + nrm(ks[4], (DEPTH, DEC_BATCH, PAST_LEN, H_A))),
        'cache_b_k': nrm(ks[5], (DEPTH, DEC_BATCH, b_rows, H_B, HEAD_DIM)),
        'cache_b_v': nrm(ks[6], (DEPTH, DEC_BATCH, b_rows, H_B, HEAD_DIM)),
        'cache_c_k': nrm(ks[7], (DEPTH, DEC_BATCH, PAST_LEN, H_C, 2, DIFF_DIM)),
        'cache_c_v': nrm(ks[8], (DEPTH, DEC_BATCH, PAST_LEN, H_C, HEAD_DIM)),
        'w_in': nrm(ks[9], (DEPTH, D_MODEL, p_width), D_MODEL ** -0.5),
        'b_f': 2.0 + nrm(ks[10], (DEPTH, H_A), 0.5),
        'g_qa': 1.0 + nrm(ks[11], (DEPTH, HEAD_DIM), 0.1),
        'g_ka': 1.0 + nrm(ks[12], (DEPTH, HEAD_DIM), 0.1),
        'g_qb': 1.0 + nrm(ks[13], (DEPTH, HEAD_DIM), 0.1),
        'g_kb': 1.0 + nrm(ks[14], (DEPTH, HEAD_DIM), 0.1),
        'rel_bias': nrm(ks[15], (DEPTH, H_B, REL_SIZE), 0.5),
        'g_qc': 1.0 + nrm(ks[16], (DEPTH, DIFF_DIM), 0.1),
        'g_kc': 1.0 + nrm(ks[17], (DEPTH, DIFF_DIM), 0.1),
        'lam_q1': nrm(ks[18], (DEPTH, DIFF_DIM), 0.1),
        'lam_k1': nrm(ks[19], (DEPTH, DIFF_DIM), 0.1),
        'lam_q2': nrm(ks[20], (DEPTH, DIFF_DIM), 0.1),
        'lam_k2': nrm(ks[21], (DEPTH, DIFF_DIM), 0.1),
        'g_subln': 1.0 + nrm(ks[22], (DEPTH, HEAD_DIM), 0.1),
        'w_out': nrm(ks[23], (DEPTH, MIX_WIDTH, D_MODEL), MIX_WIDTH ** -0.5),
        'g_mix': 1.0 + nrm(ks[24], (DEPTH, D_MODEL), 0.1),
        'g_mlp': 1.0 + nrm(ks[25], (DEPTH, D_MODEL), 0.1),
        'w_up': nrm(ks[26], (DEPTH, D_MODEL, D_FF), D_MODEL ** -0.5),
        'w_down': nrm(ks[27], (DEPTH, D_FF, D_MODEL), D_FF ** -0.5),
    }


def reference(x_prompt, x_sample, cache_a_k, cache_a_v, cache_a_logf, cache_b_k, cache_b_v,
              cache_c_k, cache_c_v, w_in, b_f, g_qa, g_ka, g_qb, g_kb, rel_bias, g_qc, g_kc,
              lam_q1, lam_k1, lam_q2, lam_k2, g_subln, w_out, g_mix, g_mlp, w_up, w_down):
    xp, xs = x_prompt, x_sample
    n_p = xp.shape[1]
    n_s = xs.shape[1]
    past = cache_a_k.shape[2]
    b_rows = cache_b_k.shape[2]
    keep_p = min(WINDOW_B, n_p)
    pos_p = jnp.arange(n_p)
    pos_s = past + jnp.arange(n_s)
    pos_s_fox_k = jnp.arange(past + n_s)
    pos_s_band_k = past - b_rows + jnp.arange(b_rows + n_s)
    pos_s_diff_k = jnp.arange(past + n_s)

    pak, pav, pal, pbk, pbv, pck, pcv = [], [], [], [], [], [], []
    sak, sav, sal, sbk, sbv, sck, scv = [], [], [], [], [], [], []
    for l in range(DEPTH):
        lam_init = 0.8 - 0.6 * math.exp(-0.3 * l)
        lam = diff_lambda(lam_q1[l], lam_k1[l], lam_q2[l], lam_k2[l], lam_init)
        proj_w = (w_in[l], b_f[l], g_qa[l], g_ka[l], g_qb[l], g_kb[l], g_qc[l], g_kc[l])

        hp = rmsnorm(xp, g_mix[l])
        qa, ka, va, la, qb, kb, vb, qc, kc, vc = project(hp, pos_p, *proj_w)
        oa = fox_prompt(qa, ka, va, la)
        ob = band_prompt(qb, kb, vb, rel_bias[l])
        oc = diff_prompt(qc, kc, vc, lam)
        xp = xp + merge_heads(oa, ob, oc, g_subln[l], lam_init, w_out[l])
        xp = xp + sqrelu_mlp(rmsnorm(xp, g_mlp[l]), w_up[l], w_down[l])
        pak.append(ka); pav.append(va); pal.append(la)
        pbk.append(kb[:, n_p - keep_p:]); pbv.append(vb[:, n_p - keep_p:])
        pck.append(kc); pcv.append(vc)

        hs = rmsnorm(xs, g_mix[l])
        qa, ka, va, la, qb, kb, vb, qc, kc, vc = project(hs, pos_s, *proj_w)
        ka_all = jnp.concatenate([cache_a_k[l], ka], axis=1)
        va_all = jnp.concatenate([cache_a_v[l], va], axis=1)
        c_all = jnp.cumsum(jnp.concatenate([cache_a_logf[l].astype(jnp.float32), la], axis=1), axis=1)
        oa = fox_attend(qa, c_all[:, past:], pos_s, ka_all, va_all, c_all, pos_s_fox_k)
        kb_all = jnp.concatenate([cache_b_k[l], kb], axis=1)
        vb_all = jnp.concatenate([cache_b_v[l], vb], axis=1)
        ob = band_attend(qb, pos_s, kb_all, vb_all, pos_s_band_k, rel_bias[l])
        kc_all = jnp.concatenate([cache_c_k[l], kc], axis=1)
        vc_all = jnp.concatenate([cache_c_v[l], vc], axis=1)
        oc = diff_attend(qc, pos_s, kc_all, vc_all, pos_s_diff_k, lam)
        xs = xs + merge_heads(oa, ob, oc, g_subln[l], lam_init, w_out[l])
        xs = xs + sqrelu_mlp(rmsnorm(xs, g_mlp[l]), w_up[l], w_down[l])
        sak.append(ka); sav.append(va); sal.append(la)
        sbk.append(kb_all[:, n_s:]); sbv.append(vb_all[:, n_s:])
        sck.append(kc); scv.append(vc)

    p_a_k = jnp.stack(pak); p_a_v = jnp.stack(pav); p_a_logf = jnp.stack(pal)
    p_b_k = jnp.stack(pbk); p_b_v = jnp.stack(pbv)
    p_c_k = jnp.stack(pck); p_c_v = jnp.stack(pcv)
    s_a_k = jnp.stack(sak); s_a_v = jnp.stack(sav); s_a_logf = jnp.stack(sal)
    s_b_k = jnp.stack(sbk); s_b_v = jnp.stack(sbv)
    s_c_k = jnp.stack(sck); s_c_v = jnp.stack(scv)
    return (xp, xs, p_a_k, p_a_v, p_a_logf, p_b_k, p_b_v, p_c_k, p_c_v,
            s_a_k, s_a_v, s_a_logf, s_b_k, s_b_v, s_c_k, s_c_v)
```

```python
import functools
import math

import jax
import jax.numpy as jnp
from jax import lax
from jax.experimental import pallas as pl
from jax.experimental.pallas import tpu as pltpu

F32 = jnp.float32
BF16 = jnp.bfloat16

CHUNK = 64
HEAD_DIM = 128
H_A = 8
H_B = 4
H_C = 4
DIFF_DIM = HEAD_DIM // 2
BAND_CHUNKS = 8
WINDOW_B = BAND_CHUNKS * CHUNK
REL_MAX_PAST = 128
ROPE_THETA = 10000.0
EPS = 1e-6
NEG = -1e30

LANES = 128
VMEM_LIMIT = 52 * 1024 * 1024

QA0, KA0, VA0 = 0, H_A, 2 * H_A
QB0 = 3 * H_A
KB0 = QB0 + H_B
QC0 = KB0 + H_B
KC0 = QC0 + H_C
VB0 = KC0 + H_C
VC0 = VB0 + H_B
P_BLOCKS = VC0 + H_C
P_WIDTH = P_BLOCKS * LANES


def _params(*sem):
    return pltpu.CompilerParams(dimension_semantics=sem, vmem_limit_bytes=VMEM_LIMIT)


def _rms_rows(x, g):
    ms = jnp.mean(x * x, axis=-1, keepdims=True)
    return x * lax.rsqrt(ms + EPS) * g


def _log_sigmoid(z):
    return jnp.minimum(z, 0.0) - jnp.log1p(jnp.exp(-jnp.abs(z)))


ROW_CHUNK = 128


def _proj_kernel(x_ref, g_ref, w_ref, wf_ref, bf_ref, gain_ref, cos_ref, sin_ref,
                 o_ref, lf_ref, h_ref, *, modes, tm, tn):
    j = pl.program_id(1)

    @pl.when(j == 0)
    def _():
        h = _rms_rows(x_ref[...], g_ref[...]).astype(BF16)
        h_ref[...] = h
        f = jnp.dot(h, wf_ref[...], preferred_element_type=F32) + bf_ref[...]
        lf_ref[...] = _log_sigmoid(f)[:, :H_A]

    o_ref[...] = jnp.dot(h_ref[...], w_ref[...], preferred_element_type=F32)

    groups = tn // LANES
    n_chunks = tm // ROW_CHUNK

    def norm128(r, carry):
        rs = pl.ds(pl.multiple_of(r * ROW_CHUNK, ROW_CHUNK), ROW_CHUNK)
        for gidx in range(groups):
            cs = slice(gidx * LANES, (gidx + 1) * LANES)
            o_ref[rs, cs] = _rms_rows(o_ref[rs, cs], gain_ref[:, cs])
        return carry

    def norm64_rope(r, carry):
        rs = pl.ds(pl.multiple_of(r * ROW_CHUNK, ROW_CHUNK), ROW_CHUNK)
        lane = lax.broadcasted_iota(jnp.int32, (ROW_CHUNK, LANES), 1)
        lo = lane < DIFF_DIM
        first = (lane % DIFF_DIM) < (DIFF_DIM // 2)
        cos = cos_ref[rs, :]
        sin = sin_ref[rs, :]
        for gidx in range(groups):
            cs = slice(gidx * LANES, (gidx + 1) * LANES)
            xg = o_ref[rs, cs]
            x2 = xg * xg
            s_lo = jnp.sum(jnp.where(lo, x2, 0.0), axis=-1, keepdims=True) * (1.0 / DIFF_DIM)
            s_hi = jnp.sum(jnp.where(lo, 0.0, x2), axis=-1, keepdims=True) * (1.0 / DIFF_DIM)
            inv = jnp.where(lo, lax.rsqrt(s_lo + EPS), lax.rsqrt(s_hi + EPS))
            y = xg * inv * gain_ref[:, cs]
            partner = jnp.where(first, pltpu.roll(y, LANES - DIFF_DIM // 2, 1),
                                pltpu.roll(y, DIFF_DIM // 2, 1))
            o_ref[rs, cs] = y * cos + partner * sin
        return carry

    for mode, fn in (("n128", norm128), ("n64r", norm64_rope)):
        js = [jj for jj, m in enumerate(modes) if m == mode]
        if not js:
            continue
        cond = functools.reduce(lambda a, b: a | b, [j == jj for jj in js])

        @pl.when(cond)
        def _(fn=fn):
            lax.fori_loop(0, n_chunks, fn, 0)


def _project(x, g_mix, w, wf, bf, gain, cos_t, sin_t, *, tm, tn):
    n, d = x.shape
    modes = []
    for jj in range(P_WIDTH // tn):
        blk = jj * tn // LANES
        if blk < VA0 or QB0 <= blk < QC0:
            modes.append("n128")
        elif QC0 <= blk < VB0:
            modes.append("n64r")
        else:
            modes.append("id")
    period = cos_t.shape[0] // tm
    kern = functools.partial(_proj_kernel, modes=tuple(modes), tm=tm, tn=tn)
    return pl.pallas_call(
        kern,
        grid=(n // tm, P_WIDTH // tn),
        in_specs=[
            pl.BlockSpec((tm, d), lambda i, j: (i, 0)),
            pl.BlockSpec((1, d), lambda i, j: (0, 0)),
            pl.BlockSpec((d, tn), lambda i, j: (0, j)),
            pl.BlockSpec((d, LANES), lambda i, j: (0, 0)),
            pl.BlockSpec((1, LANES), lambda i, j: (0, 0)),
            pl.BlockSpec((1, tn), lambda i, j: (0, j)),
            pl.BlockSpec((tm, LANES), lambda i, j: (i % period, 0)),
            pl.BlockSpec((tm, LANES), lambda i, j: (i % period, 0)),
        ],
        out_specs=[
            pl.BlockSpec((tm, tn), lambda i, j: (i, j)),
            pl.BlockSpec((tm, H_A), lambda i, j: (i, 0)),
        ],
        out_shape=[
            jax.ShapeDtypeStruct((n, P_WIDTH), F32),
            jax.ShapeDtypeStruct((n, H_A), F32),
        ],
        scratch_shapes=[pltpu.VMEM((tm, d), BF16)],
        compiler_params=_params("parallel", "arbitrary"),
        name="proj",
    )(x, g_mix, w, wf, bf, gain, cos_t, sin_t)


def _cumsum_kernel(x_ref, o_ref):
    x = x_ref[...]
    n = x.shape[1]
    lane = lax.broadcasted_iota(jnp.int32, x.shape, 1)
    s = 1
    while s < n:
        x = x + jnp.where(lane >= s, pltpu.roll(x, s, 1), 0.0)
        s *= 2
    o_ref[...] = x


def _cumsum_lanes(x):
    r, n = x.shape
    rb = 8
    return pl.pallas_call(
        _cumsum_kernel,
        grid=(r // rb,),
        in_specs=[pl.BlockSpec((rb, n), lambda i: (i, 0))],
        out_specs=pl.BlockSpec((rb, n), lambda i: (i, 0)),
        out_shape=jax.ShapeDtypeStruct((r, n), F32),
        compiler_params=_params("parallel"),
        name="cumsum",
    )(x)


def _qk(q, k):
    return lax.dot_general(q, k, (((1,), (1,)), ((), ())), preferred_element_type=F32)


def _online_step(s, v, m_ref, l_ref, acc_ref):
    m_prev = m_ref[...]
    m_new = jnp.maximum(m_prev, jnp.max(s, axis=1, keepdims=True))
    alpha = jnp.exp(m_prev - m_new)
    p = jnp.exp(s - m_new)
    l_ref[...] = alpha * l_ref[...] + jnp.sum(p, axis=1, keepdims=True)
    acc_ref[...] = alpha * acc_ref[...] + jnp.dot(p.astype(BF16), v, preferred_element_type=F32)
    m_ref[...] = m_new


def _reset(m_ref, l_ref, acc_ref):
    m_ref[...] = jnp.full(m_ref.shape, -jnp.inf, F32)
    l_ref[...] = jnp.zeros(l_ref.shape, F32)
    acc_ref[...] = jnp.zeros(acc_ref.shape, F32)


def _fox_kernel(q_ref, k_ref, v_ref, crow_ref, ccol_ref, o_ref,
                kbf, vbf, m_ref, l_ref, acc_ref, *, tq):
    i = pl.program_id(2)

    @pl.when(i == 0)
    def _():
        kbf[...] = k_ref[...].astype(BF16)
        vbf[...] = v_ref[...].astype(BF16)

    q = q_ref[...].astype(BF16)
    cq = ccol_ref[...]
    _reset(m_ref, l_ref, acc_ref)

    def scores(kb):
        ks = pl.ds(pl.multiple_of(kb * tq, tq), tq)
        s = _qk(q, kbf[ks, :]) + cq - crow_ref[pl.ds(kb, 1), :]
        return s, vbf[ks, :]

    def full_block(kb, carry):
        s, v = scores(kb)
        _online_step(s, v, m_ref, l_ref, acc_ref)
        return carry

    lax.fori_loop(0, i, full_block, 0)

    s, v = scores(i)
    row = lax.broadcasted_iota(jnp.int32, (tq, tq), 0)
    col = lax.broadcasted_iota(jnp.int32, (tq, tq), 1)
    _online_step(jnp.where(col <= row, s, NEG), v, m_ref, l_ref, acc_ref)
    o_ref[...] = (acc_ref[...] / l_ref[...]).astype(BF16)


def _fox_prompt(p3, c, *, tq):
    b, t, _ = p3.shape
    crow = c.reshape(b, H_A, t // tq, tq)
    ccol = c.reshape(b, H_A, t, 1)
    return pl.pallas_call(
        functools.partial(_fox_kernel, tq=tq),
        grid=(b, H_A, t // tq),
        in_specs=[
            pl.BlockSpec((None, tq, LANES), lambda bi, h, i: (bi, i, QA0 + h)),
            pl.BlockSpec((None, t, LANES), lambda bi, h, i: (bi, 0, KA0 + h)),
            pl.BlockSpec((None, t, LANES), lambda bi, h, i: (bi, 0, VA0 + h)),
            pl.BlockSpec((None, None, t // tq, tq), lambda bi, h, i: (bi, h, 0, 0)),
            pl.BlockSpec((None, None, tq, 1), lambda bi, h, i: (bi, h, i, 0)),
        ],
        out_specs=pl.BlockSpec((None, tq, LANES), lambda bi, h, i: (bi, i, h)),
        out_shape=jax.ShapeDtypeStruct((b, t, H_A * HEAD_DIM), BF16),
        scratch_shapes=[
            pltpu.VMEM((t, LANES), BF16), pltpu.VMEM((t, LANES), BF16),
            pltpu.VMEM((tq, 1), F32), pltpu.VMEM((tq, 1), F32), pltpu.VMEM((tq, LANES), F32),
        ],
        compiler_params=_params("parallel", "parallel", "arbitrary"),
        name="fox_prompt",
    )(p3, p3, p3, crow, ccol)


def _lambda(lq1_ref, lk1_ref, lq2_ref, lk2_ref, lam_init):
    a = jnp.sum(lq1_ref[...] * lk1_ref[...], axis=-1, keepdims=True)
    b = jnp.sum(lq2_ref[...] * lk2_ref[...], axis=-1, keepdims=True)
    return jnp.exp(a) - jnp.exp(b) + lam_init


def _stack_maps(q):
    lane = lax.broadcasted_iota(jnp.int32, q.shape, 1)
    lo = lane < DIFF_DIM
    return jnp.concatenate([jnp.where(lo, q, 0.0), jnp.where(lo, 0.0, q)], axis=0).astype(BF16)


def _diff_finish(acc, l, lam, g, lam_init, rows):
    o = acc / l
    o = o[:rows] - lam * o[rows:]
    return (_rms_rows(o, g) * (1.0 - lam_init)).astype(BF16)


def _diff_kernel(q_ref, k_ref, v_ref, lq1_ref, lk1_ref, lq2_ref, lk2_ref, g_ref, o_ref,
                 kbf, vbf, m_ref, l_ref, acc_ref, *, tq, lam_init):
    i = pl.program_id(2)

    @pl.when(i == 0)
    def _():
        kbf[...] = k_ref[...].astype(BF16)
        vbf[...] = v_ref[...].astype(BF16)

    q2 = _stack_maps(q_ref[...])
    _reset(m_ref, l_ref, acc_ref)

    def block(kb):
        ks = pl.ds(pl.multiple_of(kb * tq, tq), tq)
        return _qk(q2, kbf[ks, :]), vbf[ks, :]

    def full_block(kb, carry):
        s, v = block(kb)
        _online_step(s, v, m_ref, l_ref, acc_ref)
        return carry

    lax.fori_loop(0, i, full_block, 0)

    s, v = block(i)
    row = lax.broadcasted_iota(jnp.int32, (2 * tq, tq), 0) % tq
    col = lax.broadcasted_iota(jnp.int32, (2 * tq, tq), 1)
    s = jnp.where(col // CHUNK <= row // CHUNK, s, NEG)
    _online_step(s, v, m_ref, l_ref, acc_ref)
    lam = _lambda(lq1_ref, lk1_ref, lq2_ref, lk2_ref, lam_init)
    o_ref[...] = _diff_finish(acc_ref[...], l_ref[...], lam, g_ref[...], lam_init, tq)


def _lam_specs(nd):
    zero = (0,) * 2
    return [pl.BlockSpec((1, DIFF_DIM), lambda *a: zero) for _ in range(4)] + \
           [pl.BlockSpec((1, LANES), lambda *a: zero)]


def _diff_prompt(p3, lams, g_subln, *, tq, lam_init):
    b, t, _ = p3.shape
    return pl.pallas_call(
        functools.partial(_diff_kernel, tq=tq, lam_init=lam_init),
        grid=(b, H_C, t // tq),
        in_specs=[
            pl.BlockSpec((None, tq, LANES), lambda bi, h, i: (bi, i, QC0 + h)),
            pl.BlockSpec((None, t, LANES), lambda bi, h, i: (bi, 0, KC0 + h)),
            pl.BlockSpec((None, t, LANES), lambda bi, h, i: (bi, 0, VC0 + h)),
        ] + _lam_specs(3),
        out_specs=pl.BlockSpec((None, tq, LANES), lambda bi, h, i: (bi, i, h)),
        out_shape=jax.ShapeDtypeStruct((b, t, H_C * HEAD_DIM), BF16),
        scratch_shapes=[
            pltpu.VMEM((t, LANES), BF16), pltpu.VMEM((t, LANES), BF16),
            pltpu.VMEM((2 * tq, 1), F32), pltpu.VMEM((2 * tq, 1), F32),
            pltpu.VMEM((2 * tq, LANES), F32),
        ],
        compiler_params=_params("parallel", "parallel", "arbitrary"),
        name="diff_prompt",
    )(p3, p3, p3, *lams, g_subln)


BAND_KEYS = WINDOW_B + CHUNK


def _band_chunk(q, k, v, bias, first_key_pos):
    s = _qk(q, k) + bias
    kpos = first_key_pos + lax.broadcasted_iota(jnp.int32, s.shape, 1)
    s = jnp.where(kpos >= 0, s, NEG)
    m = jnp.max(s, axis=1, keepdims=True)
    p = jnp.exp(s - m)
    l = jnp.sum(p, axis=1, keepdims=True)
    return (jnp.dot(p.astype(BF16), v, preferred_element_type=F32) / l).astype(BF16)


def _band_kernel(q_ref, k_ref, v_ref, bias_ref, o_ref, kpad, vpad, *, tq):
    i = pl.program_id(2)

    @pl.when(i == 0)
    def _():
        kpad[:WINDOW_B, :] = jnp.zeros((WINDOW_B, LANES), BF16)
        vpad[:WINDOW_B, :] = jnp.zeros((WINDOW_B, LANES), BF16)
        kpad[WINDOW_B:, :] = k_ref[...].astype(BF16)
        vpad[WINDOW_B:, :] = v_ref[...].astype(BF16)

    bias = bias_ref[...]
    for c in range(tq // CHUNK):
        q = q_ref[c * CHUNK:(c + 1) * CHUNK, :].astype(BF16)
        start = i * tq + c * CHUNK
        ws = pl.ds(pl.multiple_of(start, CHUNK), BAND_KEYS)
        o_ref[c * CHUNK:(c + 1) * CHUNK, :] = _band_chunk(
            q, kpad[ws, :], vpad[ws, :], bias, start - WINDOW_B)


def _band_prompt(p3, bias, *, tq):
    b, t, _ = p3.shape
    return pl.pallas_call(
        functools.partial(_band_kernel, tq=tq),
        grid=(b, H_B, t // tq),
        in_specs=[
            pl.BlockSpec((None, tq, LANES), lambda bi, h, i: (bi, i, QB0 + h)),
            pl.BlockSpec((None, t, LANES), lambda bi, h, i: (bi, 0, KB0 + h)),
            pl.BlockSpec((None, t, LANES), lambda bi, h, i: (bi, 0, VB0 + h)),
            pl.BlockSpec((None, CHUNK, BAND_KEYS), lambda bi, h, i: (h, 0, 0)),
        ],
        out_specs=pl.BlockSpec((None, tq, LANES), lambda bi, h, i: (bi, i, h)),
        out_shape=jax.ShapeDtypeStruct((b, t, H_B * HEAD_DIM), BF16),
        scratch_shapes=[pltpu.VMEM((t + WINDOW_B, LANES), BF16),
                        pltpu.VMEM((t + WINDOW_B, LANES), BF16)],
        compiler_params=_params("parallel", "parallel", "arbitrary"),
        name="band_prompt",
    )(p3, p3, p3, bias)


def _two_part_softmax(s_c, s_n, v_c, v_n):
    m = jnp.maximum(jnp.max(s_c, axis=1, keepdims=True), jnp.max(s_n, axis=1, keepdims=True))
    p_c = jnp.exp(s_c - m)
    p_n = jnp.exp(s_n - m)
    l = jnp.sum(p_c, axis=1, keepdims=True) + jnp.sum(p_n, axis=1, keepdims=True)
    acc = jnp.dot(p_c.astype(BF16), v_c, preferred_element_type=F32) + \
        jnp.dot(p_n.astype(BF16), v_n, preferred_element_type=F32)
    return acc, l


def _fox_sample_kernel(q_ref, kn_ref, vn_ref, kc_ref, vc_ref, crow_ref, ccol_ref, o_ref, *, past):
    q = q_ref[...].astype(BF16)
    n = q.shape[0]
    cq = ccol_ref[...]
    s_c = _qk(q, kc_ref[...].astype(BF16)) + cq - crow_ref[:, :past]
    s_n = _qk(q, kn_ref[...].astype(BF16)) + cq - crow_ref[:, past:past + n]
    row = lax.broadcasted_iota(jnp.int32, (n, n), 0)
    col = lax.broadcasted_iota(jnp.int32, (n, n), 1)
    s_n = jnp.where(col <= row, s_n, NEG)
    acc, l = _two_part_softmax(s_c, s_n, vc_ref[...].astype(BF16), vn_ref[...].astype(BF16))
    o_ref[...] = (acc / l).astype(BF16)


def _fox_sample(p3, ck, cv, crow, ccol):
    b, n, _ = p3.shape
    past = ck.shape[1]
    width = crow.shape[-1]
    return pl.pallas_call(
        functools.partial(_fox_sample_kernel, past=past),
        grid=(b, H_A),
        in_specs=[
            pl.BlockSpec((None, n, LANES), lambda bi, h: (bi, 0, QA0 + h)),
            pl.BlockSpec((None, n, LANES), lambda bi, h: (bi, 0, KA0 + h)),
            pl.BlockSpec((None, n, LANES), lambda bi, h: (bi, 0, VA0 + h)),
            pl.BlockSpec((None, past, LANES), lambda bi, h: (bi, 0, h)),
            pl.BlockSpec((None, past, LANES), lambda bi, h: (bi, 0, h)),
            pl.BlockSpec((None, None, 1, width), lambda bi, h: (bi, h, 0, 0)),
            pl.BlockSpec((None, None, n, 1), lambda bi, h: (bi, h, 0, 0)),
        ],
        out_specs=pl.BlockSpec((None, n, LANES), lambda bi, h: (bi, 0, h)),
        out_shape=jax.ShapeDtypeStruct((b, n, H_A * HEAD_DIM), BF16),
        compiler_params=_params("parallel", "parallel"),
        name="fox_sample",
    )(p3, p3, p3, ck, cv, crow, ccol)


def _band_sample_kernel(q_ref, kn_ref, vn_ref, kc_ref, vc_ref, bias_ref, o_ref, *, first_key_pos):
    k = jnp.concatenate([kc_ref[...], kn_ref[...]], axis=0).astype(BF16)
    v = jnp.concatenate([vc_ref[...], vn_ref[...]], axis=0).astype(BF16)
    o_ref[...] = _band_chunk(q_ref[...].astype(BF16), k, v, bias_ref[...], first_key_pos)


def _band_sample(p3, ck, cv, bias, *, first_key_pos):
    b, n, _ = p3.shape
    rows = ck.shape[1]
    return pl.pallas_call(
        functools.partial(_band_sample_kernel, first_key_pos=first_key_pos),
        grid=(b, H_B),
        in_specs=[
            pl.BlockSpec((None, n, LANES), lambda bi, h: (bi, 0, QB0 + h)),
            pl.BlockSpec((None, n, LANES), lambda bi, h: (bi, 0, KB0 + h)),
            pl.BlockSpec((None, n, LANES), lambda bi, h: (bi, 0, VB0 + h)),
            pl.BlockSpec((None, rows, LANES), lambda bi, h: (bi, 0, h)),
            pl.BlockSpec((None, rows, LANES), lambda bi, h: (bi, 0, h)),
            pl.BlockSpec((None, CHUNK, BAND_KEYS), lambda bi, h: (h, 0, 0)),
        ],
        out_specs=pl.BlockSpec((None, n, LANES), lambda bi, h: (bi, 0, h)),
        out_shape=jax.ShapeDtypeStruct((b, n, H_B * HEAD_DIM), BF16),
        compiler_params=_params("parallel", "parallel"),
        name="band_sample",
    )(p3, p3, p3, ck, cv, bias)


def _diff_sample_kernel(q_ref, kn_ref, vn_ref, kc_ref, vc_ref, lq1_ref, lk1_ref, lq2_ref, lk2_ref,
                        g_ref, o_ref, *, lam_init):
    n = q_ref.shape[0]
    q2 = _stack_maps(q_ref[...])
    s_c = _qk(q2, kc_ref[...].astype(BF16))
    s_n = _qk(q2, kn_ref[...].astype(BF16))
    acc, l = _two_part_softmax(s_c, s_n, vc_ref[...].astype(BF16), vn_ref[...].astype(BF16))
    lam = _lambda(lq1_ref, lk1_ref, lq2_ref, lk2_ref, lam_init)
    o_ref[...] = _diff_finish(acc, l, lam, g_ref[...], lam_init, n)


def _diff_sample(p3, ck, cv, lams, g_subln, *, lam_init):
    b, n, _ = p3.shape
    past = ck.shape[1]
    return pl.pallas_call(
        functools.partial(_diff_sample_kernel, lam_init=lam_init),
        grid=(b, H_C),
        in_specs=[
            pl.BlockSpec((None, n, LANES), lambda bi, h: (bi, 0, QC0 + h)),
            pl.BlockSpec((None, n, LANES), lambda bi, h: (bi, 0, KC0 + h)),
            pl.BlockSpec((None, n, LANES), lambda bi, h: (bi, 0, VC0 + h)),
            pl.BlockSpec((None, past, LANES), lambda bi, h: (bi, 0, h)),
            pl.BlockSpec((None, past, LANES), lambda bi, h: (bi, 0, h)),
        ] + _lam_specs(2),
        out_specs=pl.BlockSpec((None, n, LANES), lambda bi, h: (bi, 0, h)),
        out_shape=jax.ShapeDtypeStruct((b, n, H_C * HEAD_DIM), BF16),
        compiler_params=_params("parallel", "parallel"),
        name="diff_sample",
    )(p3, p3, p3, ck, cv, *lams, g_subln)


def _merge_kernel(x_ref, oa_ref, ob_ref, oc_ref, w_ref, o_ref):
    o = jnp.concatenate([oa_ref[...], ob_ref[...], oc_ref[...]], axis=1)
    o_ref[...] = x_ref[...] + jnp.dot(o, w_ref[...], preferred_element_type=F32)


def _merge(x, oa, ob, oc, w, *, tm):
    n, d = x.shape
    mix = w.shape[0]
    return pl.pallas_call(
        _merge_kernel,
        grid=(n // tm,),
        in_specs=[
            pl.BlockSpec((tm, d), lambda i: (i, 0)),
            pl.BlockSpec((tm, oa.shape[1]), lambda i: (i, 0)),
            pl.BlockSpec((tm, ob.shape[1]), lambda i: (i, 0)),
            pl.BlockSpec((tm, oc.shape[1]), lambda i: (i, 0)),
            pl.BlockSpec((mix, d), lambda i: (0, 0)),
        ],
        out_specs=pl.BlockSpec((tm, d), lambda i: (i, 0)),
        out_shape=jax.ShapeDtypeStruct((n, d), F32),
        compiler_params=_params("parallel"),
        name="merge",
    )(x, oa, ob, oc, w)


def _mlp_kernel(x_ref, g_ref, wu_ref, wd_ref, o_ref, h_ref):
    @pl.when(pl.program_id(1) == 0)
    def _():
        x = x_ref[...]
        h_ref[...] = _rms_rows(x, g_ref[...]).astype(BF16)
        o_ref[...] = x

    u = jnp.dot(h_ref[...], wu_ref[...], preferred_element_type=F32)
    a = jnp.square(jnp.maximum(u, 0.0)).astype(BF16)
    o_ref[...] += jnp.dot(a, wd_ref[...], preferred_element_type=F32)


def _mlp(x, g, wu, wd, *, tm, tf):
    n, d = x.shape
    ff = wu.shape[1]
    return pl.pallas_call(
        _mlp_kernel,
        grid=(n // tm, ff // tf),
        in_specs=[
            pl.BlockSpec((tm, d), lambda i, f: (i, 0)),
            pl.BlockSpec((1, d), lambda i, f: (0, 0)),
            pl.BlockSpec((d, tf), lambda i, f: (0, f)),
            pl.BlockSpec((tf, d), lambda i, f: (f, 0)),
        ],
        out_specs=pl.BlockSpec((tm, d), lambda i, f: (i, 0)),
        out_shape=jax.ShapeDtypeStruct((n, d), F32),
        scratch_shapes=[pltpu.VMEM((tm, d), BF16)],
        compiler_params=_params("parallel", "arbitrary"),
        name="mlp",
    )(x, g, wu, wd)


def _split_cols(w):
    sizes = [H_A * HEAD_DIM] * 3 + [H_A] + [H_B * HEAD_DIM] * 3 + [H_C * HEAD_DIM] * 3
    out, c = [], 0
    for s in sizes:
        out.append(w[..., c:c + s])
        c += s
    return out


def _rope_tables(pos):
    half = DIFF_DIM // 2
    inv = ROPE_THETA ** (-jnp.arange(half, dtype=F32) * 2.0 / DIFF_DIM)
    ang = pos.astype(F32)[:, None] * inv[None, :]
    cos, sin = jnp.cos(ang), jnp.sin(ang)
    cos_t = jnp.tile(cos, (1, LANES // half))
    sin_t = jnp.tile(jnp.concatenate([-sin, sin], axis=1), (1, LANES // DIFF_DIM))
    return cos_t, sin_t


def _band_bias(rel_table):
    qi = jnp.arange(CHUNK)[:, None]
    kj = jnp.arange(BAND_KEYS)[None, :]
    idx = jnp.clip(qi - kj + WINDOW_B, -(CHUNK - 1), REL_MAX_PAST) + (CHUNK - 1)
    return rel_table.astype(F32)[:, idx]


def _pick(n, pref):
    return pref if n % pref == 0 else n


def kernel(x_prompt, x_sample, cache_a_k, cache_a_v, cache_a_logf, cache_b_k, cache_b_v, cache_c_k, cache_c_v, w_in, b_f, g_qa, g_ka, g_qb, g_kb, rel_bias, g_qc, g_kc, lam_q1, lam_k1, lam_q2, lam_k2, g_subln, w_out, g_mix, g_mlp, w_up, w_down):
    depth = w_in.shape[0]
    bp, t, d = x_prompt.shape
    bs, ns, _ = x_sample.shape
    past = cache_a_k.shape[2]
    b_rows = cache_b_k.shape[2]
    keep_p = min(WINDOW_B, t)
    assert ns == CHUNK and b_rows == WINDOW_B and past % CHUNK == 0 and t % WINDOW_B == 0
    n_p, n_s = bp * t, bs * ns

    tm_p = _pick(n_p, 1024)
    tm_s = _pick(n_s, 1024)
    tn = 1024
    tq_fox = _pick(t, 512)
    tq_diff = _pick(t, 256)
    tq_band = WINDOW_B

    cos_p, sin_p = _rope_tables(jnp.arange(t))
    cos_s, sin_s = _rope_tables(past + jnp.arange(ns))
    cos_s = jnp.tile(cos_s, (tm_s // ns, 1))
    sin_s = jnp.tile(sin_s, (tm_s // ns, 1))
    c_width = -(-(past + ns) // LANES) * LANES

    xp = x_prompt.reshape(n_p, d)
    xs = x_sample.reshape(n_s, d)
    outs = [[] for _ in range(14)]
    ones = jnp.ones((HEAD_DIM,), F32)

    for l in range(depth):
        lam_init = 0.8 - 0.6 * math.exp(-0.3 * l)
        qa, ka, va, fa, qb, kb, vb, qc, kc, vc = _split_cols(w_in[l])
        w = jnp.concatenate([qa, ka, va, qb, kb, qc, kc, vb, vc], axis=1).astype(BF16)
        wf = jnp.pad(fa, ((0, 0), (0, LANES - H_A))).astype(BF16)
        bf = jnp.pad(b_f[l], (0, LANES - H_A)).reshape(1, LANES)
        gqc = jnp.tile(g_qc[l], 2)
        gkc = jnp.tile(g_kc[l], 2)
        gain = jnp.concatenate(
            [jnp.tile(g_qa[l] * HEAD_DIM ** -0.5, H_A), jnp.tile(g_ka[l], H_A), jnp.tile(ones, H_A),
             jnp.tile(g_qb[l] * HEAD_DIM ** -0.5, H_B), jnp.tile(g_kb[l], H_B),
             jnp.tile(gqc * DIFF_DIM ** -0.5, H_C), jnp.tile(gkc, H_C),
             jnp.tile(ones, H_B + H_C)]).reshape(1, P_WIDTH)
        gmix = g_mix[l].reshape(1, d)
        gmlp = g_mlp[l].reshape(1, d)
        gsub = g_subln[l].reshape(1, HEAD_DIM)
        lams = [a[l].reshape(1, DIFF_DIM) for a in (lam_q1, lam_k1, lam_q2, lam_k2)]
        wo = w_out[l].astype(BF16)
        wu = w_up[l].astype(BF16)
        wd = w_down[l].astype(BF16)
        bias = _band_bias(rel_bias[l])

        pp, lf_p = _project(xp, gmix, w, wf, bf, gain, cos_p, sin_p, tm=tm_p, tn=tn)
        p3 = pp.reshape(bp, t, P_WIDTH)
        logf_rows = jnp.transpose(lf_p.reshape(bp, t, H_A), (0, 2, 1)).reshape(bp * H_A, t)
        c_p = _cumsum_lanes(logf_rows)
        oa = _fox_prompt(p3, c_p, tq=tq_fox)
        ob = _band_prompt(p3, bias, tq=tq_band)
        oc = _diff_prompt(p3, lams, gsub, tq=tq_diff, lam_init=lam_init)
        xp = _merge(xp, oa.reshape(n_p, -1), ob.reshape(n_p, -1), oc.reshape(n_p, -1), wo, tm=_pick(n_p, 512))
        xp = _mlp(xp, gmlp, wu, wd, tm=_pick(n_p, 512), tf=1024)

        def cols(p, start, heads):
            return p[:, :, start * LANES:(start + heads) * LANES]

        outs[0].append(cols(p3, KA0, H_A).reshape(bp, t, H_A, HEAD_DIM))
        outs[1].append(cols(p3, VA0, H_A).reshape(bp, t, H_A, HEAD_DIM))
        outs[2].append(lf_p.reshape(bp, t, H_A))
        outs[3].append(cols(p3, KB0, H_B)[:, t - keep_p:].reshape(bp, keep_p, H_B, HEAD_DIM))
        outs[4].append(cols(p3, VB0, H_B)[:, t - keep_p:].reshape(bp, keep_p, H_B, HEAD_DIM))
        outs[5].append(cols(p3, KC0, H_C).reshape(bp, t, H_C, 2, DIFF_DIM))
        outs[6].append(cols(p3, VC0, H_C).reshape(bp, t, H_C, HEAD_DIM))

        ps, lf_s = _project(xs, gmix, w, wf, bf, gain, cos_s, sin_s, tm=tm_s, tn=tn)
        s3 = ps.reshape(bs, ns, P_WIDTH)
        la = lf_s.reshape(bs, ns, H_A)
        logf_all = jnp.concatenate(
            [cache_a_logf[l].astype(F32), la, jnp.zeros((bs, c_width - past - ns, H_A), F32)], axis=1)
        c_s = _cumsum_lanes(jnp.transpose(logf_all, (0, 2, 1)).reshape(bs * H_A, c_width))
        crow = c_s.reshape(bs, H_A, 1, c_width)
        ccol = c_s[:, past:past + ns].reshape(bs, H_A, ns, 1)
        oa = _fox_sample(s3, cache_a_k[l].reshape(bs, past, -1), cache_a_v[l].reshape(bs, past, -1), crow, ccol)
        ob = _band_sample(s3, cache_b_k[l].reshape(bs, b_rows, -1), cache_b_v[l].reshape(bs, b_rows, -1),
                          bias, first_key_pos=past - b_rows)
        oc = _diff_sample(s3, cache_c_k[l].reshape(bs, past, -1), cache_c_v[l].reshape(bs, past, -1),
                          lams, gsub, lam_init=lam_init)
        xs = _merge(xs, oa.reshape(n_s, -1), ob.reshape(n_s, -1), oc.reshape(n_s, -1), wo, tm=_pick(n_s, 512))
        xs = _mlp(xs, gmlp, wu, wd, tm=_pick(n_s, 512), tf=1024)

        kb_new = cols(s3, KB0, H_B).reshape(bs, ns, H_B, HEAD_DIM)
        vb_new = cols(s3, VB0, H_B).reshape(bs, ns, H_B, HEAD_DIM)
        outs[7].append(cols(s3, KA0, H_A).reshape(bs, ns, H_A, HEAD_DIM))
        outs[8].append(cols(s3, VA0, H_A).reshape(bs, ns, H_A, HEAD_DIM))
        outs[9].append(la)
        outs[10].append(jnp.concatenate([cache_b_k[l][:, ns:], kb_new], axis=1))
        outs[11].append(jnp.concatenate([cache_b_v[l][:, ns:], vb_new], axis=1))
        outs[12].append(cols(s3, KC0, H_C).reshape(bs, ns, H_C, 2, DIFF_DIM))
        outs[13].append(cols(s3, VC0, H_C).reshape(bs, ns, H_C, HEAD_DIM))

    return (xp.reshape(bp, t, d), xs.reshape(bs, ns, d)) + tuple(jnp.stack(o) for o in outs)
```

```python
import functools
import math

import jax
import jax.numpy as jnp
from jax import lax
from jax.experimental import pallas as pl
from jax.experimental.pallas import tpu as pltpu

F32 = jnp.float32
BF16 = jnp.bfloat16

CHUNK = 64
HEAD_DIM = 128
H_A = 8
H_B = 4
H_C = 4
DIFF_DIM = HEAD_DIM // 2
BAND_CHUNKS = 8
WINDOW_B = BAND_CHUNKS * CHUNK
BAND_KEYS = WINDOW_B + CHUNK
REL_MAX_PAST = 128
ROPE_THETA = 10000.0
EPS = 1e-6
NEG = -1e30
LOG2E = math.log2(math.e)

LANES = 128
MXU_DIM = 256
VMEM_LIMIT = 52 * 1024 * 1024

WA, WB, WC = H_A * HEAD_DIM, H_B * HEAD_DIM, H_C * HEAD_DIM
Q_WIDTH = WA + WB + WC
QB_BLK = H_A
QC_BLK = H_A + H_B
P_WIDTH = 3 * (WA + WB + WC)
TN = 1024


def _params(*sem):
    return pltpu.CompilerParams(dimension_semantics=sem, vmem_limit_bytes=VMEM_LIMIT)


def _rms_rows(x, g):
    ms = jnp.mean(x * x, axis=-1, keepdims=True)
    return x * lax.rsqrt(ms + EPS) * g


def _log_sigmoid(z):
    return jnp.minimum(z, 0.0) - jnp.log1p(jnp.exp(-jnp.abs(z)))


ROW_CHUNK = 256


def _proj_kernel(*refs, tm, layer, n_alias):
    (x_ref, g_ref, w_ref, wf_ref, bf_ref, gain_ref, cos_ref, sin_ref, g128_ref, g64_ref) = refs[:10]
    (q_ref, kbvb_ref, ka_ref, va_ref, kc_ref, vc_ref, lf_ref, h_ref, acc_ref) = refs[10 + n_alias:]
    j = pl.program_id(1)

    @pl.when(j == 0)
    def _():
        h = _rms_rows(x_ref[...], g_ref[...]).astype(BF16)
        h_ref[...] = h
        f = jnp.dot(h, wf_ref[...], preferred_element_type=F32) + bf_ref[...]
        lf_ref[...] = _log_sigmoid(f)[:, :H_A]

    acc_ref[...] = jnp.dot(h_ref[...], w_ref[...], preferred_element_type=F32)

    def finish(c0, c1, mode, dst_ref, d0):
        for r0 in range(0, tm, ROW_CHUNK):
            rs = slice(r0, r0 + ROW_CHUNK)
            for c in range(c0, c1, MXU_DIM):
                cs = slice(c, c + MXU_DIM)
                x = acc_ref[rs, cs]
                if mode == "id":
                    y = x
                else:
                    gm, hd = (g128_ref, HEAD_DIM) if mode == "n128" else (g64_ref, DIFF_DIM)
                    ssq = jnp.dot((x * x).astype(BF16), gm[...], preferred_element_type=F32)
                    y = x * lax.rsqrt(ssq * (1.0 / hd) + EPS) * gain_ref[:, cs]
                if mode == "n64r":
                    lane = lax.broadcasted_iota(jnp.int32, y.shape, 1)
                    first = (lane % DIFF_DIM) < (DIFF_DIM // 2)
                    cos = jnp.concatenate([cos_ref[rs, :]] * (MXU_DIM // LANES), axis=1)
                    sin = jnp.concatenate([sin_ref[rs, :]] * (MXU_DIM // LANES), axis=1)
                    partner = jnp.where(first, pltpu.roll(y, MXU_DIM - DIFF_DIM // 2, 1),
                                        pltpu.roll(y, DIFF_DIM // 2, 1))
                    y = y * cos + partner * sin
                dst_ref[rs, d0 + c - c0:d0 + c - c0 + MXU_DIM] = y.astype(dst_ref.dtype)

    plan = (
        ((0, WA, "n128", q_ref, 0),),
        ((0, WB, "n128", q_ref, WA), (WB, WB + WC, "n64r", q_ref, WA + WB)),
        ((0, WA, "n128", ka_ref, 0),),
        ((0, WA, "id", va_ref, 0),),
        ((0, WB, "n128", kbvb_ref, 0), (WB, 2 * WB, "id", kbvb_ref, WB)),
        ((0, WC, "n64r", kc_ref, 0), (WC, 2 * WC, "id", vc_ref, 0)),
    )
    for jj, pieces in enumerate(plan):
        @pl.when(j == jj)
        def _(pieces=pieces):
            for piece in pieces:
                finish(*piece)


def _project(x, g_mix, prm, cos_t, sin_t, bufs, *, tm, layer, depth):
    n, d = x.shape
    period = cos_t.shape[0] // tm
    n_alias = 0 if bufs is None else len(bufs)
    stack = lambda width: jax.ShapeDtypeStruct((depth, n, width), F32)
    lay = lambda width: pl.BlockSpec((None, tm, width), lambda i, j: (layer, i, 0))
    const = lambda shape: pl.BlockSpec(shape, lambda i, j: (0,) * len(shape))
    in_specs = [
        pl.BlockSpec((tm, d), lambda i, j: (i, 0)),
        const((1, d)),
        pl.BlockSpec((d, TN), lambda i, j: (0, j)),
        const((d, LANES)),
        const((1, LANES)),
        pl.BlockSpec((1, TN), lambda i, j: (0, j)),
        pl.BlockSpec((tm, LANES), lambda i, j: (i % period, 0)),
        pl.BlockSpec((tm, LANES), lambda i, j: (i % period, 0)),
        const((MXU_DIM, MXU_DIM)),
        const((MXU_DIM, MXU_DIM)),
    ] + [pl.BlockSpec(memory_space=pl.ANY)] * n_alias
    out = pl.pallas_call(
        functools.partial(_proj_kernel, tm=tm, layer=layer, n_alias=n_alias),
        grid=(n // tm, P_WIDTH // TN),
        in_specs=in_specs,
        out_specs=[
            pl.BlockSpec((tm, Q_WIDTH), lambda i, j: (i, 0)),
            pl.BlockSpec((tm, 2 * WB), lambda i, j: (i, 0)),
            lay(WA), lay(WA), lay(WC), lay(WC), lay(H_A),
        ],
        out_shape=[
            jax.ShapeDtypeStruct((n, Q_WIDTH), BF16),
            jax.ShapeDtypeStruct((n, 2 * WB), F32),
            stack(WA), stack(WA), stack(WC), stack(WC), stack(H_A),
        ],
        input_output_aliases={10 + k: 2 + k for k in range(n_alias)},
        scratch_shapes=[pltpu.VMEM((tm, d), BF16), pltpu.VMEM((tm, TN), F32)],
        compiler_params=_params("parallel", "arbitrary"),
        name="proj",
    )(x, g_mix, prm["w"], prm["wf"], prm["bf"], prm["gain"], cos_t, sin_t, prm["g128"], prm["g64"],
      *(bufs or ()))
    return out[0], out[1], tuple(out[2:])


def _cumsum_kernel(x_ref, o_ref):
    x = x_ref[...]
    n = x.shape[1]
    lane = lax.broadcasted_iota(jnp.int32, x.shape, 1)
    s = 1
    while s < n:
        x = x + jnp.where(lane >= s, pltpu.roll(x, s, 1), 0.0)
        s *= 2
    o_ref[...] = x


def _cumsum_lanes(x):
    r, n = x.shape
    rb = 8
    return pl.pallas_call(
        _cumsum_kernel,
        grid=(r // rb,),
        in_specs=[pl.BlockSpec((rb, n), lambda i: (i, 0))],
        out_specs=pl.BlockSpec((rb, n), lambda i: (i, 0)),
        out_shape=jax.ShapeDtypeStruct((r, n), F32),
        compiler_params=_params("parallel"),
        name="cumsum",
    )(x)


V_ROWS = HEAD_DIM + 16


def _flash_t(q, kbf, vt, m_ref, acc_ref, s_ref, mask_fn, *, tk, n_full):
    m_ref[...] = jnp.full(m_ref.shape, -jnp.inf, F32)
    acc_ref[...] = jnp.zeros(acc_ref.shape, F32)

    def scores(kb, slot):
        ks = pl.ds(pl.multiple_of(kb * tk, tk), tk)
        s_ref[slot] = lax.dot_general(kbf[ks, :], q, (((1,), (1,)), ((), ())), preferred_element_type=F32)

    def softmax_pv(kb, slot, masked):
        st = s_ref[slot]
        if masked:
            st = mask_fn(st)
        m_prev = m_ref[...]
        m_new = jnp.maximum(m_prev, jnp.max(st, axis=0, keepdims=True))
        m_ref[...] = m_new
        pt = jnp.exp2(st - m_new).astype(BF16)
        alpha = jnp.exp2(m_prev - m_new)
        acc_ref[...] = alpha * acc_ref[...] + jnp.dot(vt[kb], pt, preferred_element_type=F32)

    scores(0, 0)

    def pair(jp, carry):
        scores(2 * jp + 1, 1)
        softmax_pv(2 * jp, 0, False)
        scores(2 * jp + 2, 0)
        softmax_pv(2 * jp + 1, 1, False)
        return carry

    lax.fori_loop(0, n_full // 2, pair, 0)
    odd = n_full % 2 == 1

    @pl.when(odd)
    def _():
        scores(n_full, 1)
        softmax_pv(n_full - 1, 0, False)
        softmax_pv(n_full, 1, True)

    @pl.when(jnp.logical_not(odd))
    def _():
        softmax_pv(n_full, 0, True)


def _fill_vt(vt, v_ref, tk):
    ones_row = jnp.where(lax.broadcasted_iota(jnp.int32, (V_ROWS - HEAD_DIM, tk), 0) == 0, 1.0, 0.0)
    for kb in range(vt.shape[0]):
        vt[kb, :HEAD_DIM, :] = v_ref[kb * tk:(kb + 1) * tk, :].T.astype(BF16)
        vt[kb, HEAD_DIM:, :] = ones_row.astype(BF16)


def _split3(c):
    hi = c.astype(BF16).astype(F32)
    r1 = c - hi
    mid = r1.astype(BF16).astype(F32)
    lo = (r1 - mid).astype(BF16).astype(F32)
    return hi, mid, lo


def _decay_lanes(c_rep, key_side):
    hi, mid, lo = _split3(c_rep)
    lane = lax.broadcasted_iota(jnp.int32, c_rep.shape, 1)
    if key_side:
        parts = (-hi, -mid, -lo, 1.0, 1.0, 1.0)
    else:
        parts = (1.0, 1.0, 1.0, hi, mid, lo)
    vals = jnp.zeros(c_rep.shape, F32)
    for idx, part in enumerate(parts):
        vals = jnp.where(lane == idx, part, vals)
    return vals.astype(BF16)


def _fox_kernel(q_ref, k_ref, v_ref, crow_ref, o_ref, kbf, vt, crep, m_ref, acc_ref, s_ref, *, tq):
    i = pl.program_id(2)

    @pl.when(i == 0)
    def _():
        kbf[:, :HEAD_DIM] = k_ref[...].astype(BF16)
        _fill_vt(vt, v_ref, tq)
        for kb in range(vt.shape[0]):
            rs = slice(kb * tq, (kb + 1) * tq)
            c_rep = jnp.broadcast_to(crow_ref[:, rs] * LOG2E, (LANES, tq)).T
            crep[rs, :] = c_rep
            kbf[rs, HEAD_DIM:] = _decay_lanes(c_rep, True)

    cq_rep = crep[pl.ds(pl.multiple_of(i * tq, tq), tq), :]
    q = jnp.concatenate([q_ref[...], _decay_lanes(cq_rep, False)], axis=1)

    def mask_fn(st):
        kpos = lax.broadcasted_iota(jnp.int32, st.shape, 0)
        qpos = lax.broadcasted_iota(jnp.int32, st.shape, 1)
        return jnp.where(kpos <= qpos, st, NEG)

    _flash_t(q, kbf, vt, m_ref, acc_ref, s_ref, mask_fn, tk=tq, n_full=i)
    acc = acc_ref[...]
    o_ref[...] = (acc[:HEAD_DIM] / acc[HEAD_DIM:HEAD_DIM + 1]).T.astype(BF16)


def _fox_prompt(q3, ka, va, c, *, layer, tq):
    b, t, _ = q3.shape
    crow = c.reshape(b, H_A, 1, t)
    kv = lambda: pl.BlockSpec((None, None, t, LANES), lambda bi, h, i: (layer, bi, 0, h))
    return pl.pallas_call(
        functools.partial(_fox_kernel, tq=tq),
        grid=(b, H_A, t // tq),
        in_specs=[
            pl.BlockSpec((None, tq, LANES), lambda bi, h, i: (bi, i, h)),
            kv(), kv(),
            pl.BlockSpec((None, None, 1, t), lambda bi, h, i: (bi, h, 0, 0)),
        ],
        out_specs=pl.BlockSpec((None, tq, LANES), lambda bi, h, i: (bi, i, h)),
        out_shape=jax.ShapeDtypeStruct((b, t, WA), BF16),
        scratch_shapes=[
            pltpu.VMEM((t, 2 * HEAD_DIM), BF16), pltpu.VMEM((t // tq, V_ROWS, tq), BF16),
            pltpu.VMEM((t, LANES), F32),
            pltpu.VMEM((1, tq), F32), pltpu.VMEM((V_ROWS, tq), F32), pltpu.VMEM((2, tq, tq), F32),
        ],
        compiler_params=_params("parallel", "parallel", "arbitrary"),
        name="fox_prompt",
    )(q3, ka, va, crow)


def _lambda(lq1_ref, lk1_ref, lq2_ref, lk2_ref, lam_init):
    a = jnp.sum(lq1_ref[...] * lk1_ref[...], axis=-1, keepdims=True)
    b = jnp.sum(lq2_ref[...] * lk2_ref[...], axis=-1, keepdims=True)
    return jnp.exp(a) - jnp.exp(b) + lam_init


def _stack_maps(q):
    lane = lax.broadcasted_iota(jnp.int32, q.shape, 1)
    lo = lane < DIFF_DIM
    zero = jnp.zeros_like(q)
    return jnp.concatenate([jnp.where(lo, q, zero), jnp.where(lo, zero, q)], axis=0)


def _subln(o, g, lam_init):
    return (_rms_rows(o, g) * (1.0 - lam_init)).astype(BF16)


def _diff_kernel(q_ref, k_ref, v_ref, lq1_ref, lk1_ref, lq2_ref, lk2_ref, g_ref, o_ref,
                 kbf, vt, m_ref, acc_ref, s_ref, *, tq, lam_init):
    i = pl.program_id(2)

    @pl.when(i == 0)
    def _():
        kbf[...] = k_ref[...].astype(BF16)
        _fill_vt(vt, v_ref, tq)

    q2 = _stack_maps(q_ref[...])

    def mask_fn(st):
        kpos = lax.broadcasted_iota(jnp.int32, st.shape, 0)
        qpos = lax.broadcasted_iota(jnp.int32, st.shape, 1) % tq
        return jnp.where(kpos // CHUNK <= qpos // CHUNK, st, NEG)

    _flash_t(q2, kbf, vt, m_ref, acc_ref, s_ref, mask_fn, tk=tq, n_full=i)
    acc = acc_ref[...]
    o = acc[:HEAD_DIM] / acc[HEAD_DIM:HEAD_DIM + 1]
    lam = _lambda(lq1_ref, lk1_ref, lq2_ref, lk2_ref, lam_init)
    o_ref[...] = _subln((o[:, :tq] - lam * o[:, tq:]).T, g_ref[...], lam_init)


def _lam_specs():
    return [pl.BlockSpec((1, DIFF_DIM), lambda *a: (0, 0)) for _ in range(4)] + \
           [pl.BlockSpec((1, LANES), lambda *a: (0, 0))]


def _diff_prompt(q3, kc, vc, lams, g_subln, *, layer, tq, lam_init):
    b, t, _ = q3.shape
    kv = lambda: pl.BlockSpec((None, None, t, LANES), lambda bi, h, i: (layer, bi, 0, h))
    return pl.pallas_call(
        functools.partial(_diff_kernel, tq=tq, lam_init=lam_init),
        grid=(b, H_C, t // tq),
        in_specs=[pl.BlockSpec((None, tq, LANES), lambda bi, h, i: (bi, i, QC_BLK + h)), kv(), kv()]
        + _lam_specs(),
        out_specs=pl.BlockSpec((None, tq, LANES), lambda bi, h, i: (bi, i, h)),
        out_shape=jax.ShapeDtypeStruct((b, t, WC), BF16),
        scratch_shapes=[
            pltpu.VMEM((t, HEAD_DIM), BF16), pltpu.VMEM((t // tq, V_ROWS, tq), BF16),
            pltpu.VMEM((1, 2 * tq), F32), pltpu.VMEM((V_ROWS, 2 * tq), F32),
            pltpu.VMEM((2, tq, 2 * tq), F32),
        ],
        compiler_params=_params("parallel", "parallel", "arbitrary"),
        name="diff_prompt",
    )(q3, kc, vc, *lams, g_subln)


BIAS_EXT = 5 * LANES


def _band_bias_tile(ext_ref):
    x = jnp.broadcast_to(ext_ref[...], (CHUNK, BIAS_EXT))
    return pltpu.roll(x, BIAS_EXT - (CHUNK - 1), 1, stride=1, stride_axis=0)[:, :BAND_KEYS]


def _band_chunk(q, k, v, bias, first_key_pos):
    s = lax.dot_general(q, k, (((1,), (1,)), ((), ())), preferred_element_type=F32) + bias
    kpos = first_key_pos + lax.broadcasted_iota(jnp.int32, s.shape, 1)
    s = jnp.where(kpos >= 0, s, NEG)
    m = jnp.max(s, axis=1, keepdims=True)
    p = jnp.exp2(s - m)
    l = jnp.sum(p, axis=1, keepdims=True)
    return (jnp.dot(p.astype(BF16), v, preferred_element_type=F32) / l).astype(BF16)


def _band_kernel(q_ref, k_ref, v_ref, ext_ref, o_ref, kpad, vpad, bias_ref, *, tq):
    i = pl.program_id(2)

    @pl.when(i == 0)
    def _():
        kpad[:WINDOW_B, :] = jnp.zeros((WINDOW_B, LANES), BF16)
        vpad[:WINDOW_B, :] = jnp.zeros((WINDOW_B, LANES), BF16)
        kpad[WINDOW_B:, :] = k_ref[...].astype(BF16)
        vpad[WINDOW_B:, :] = v_ref[...].astype(BF16)
        bias_ref[...] = _band_bias_tile(ext_ref)

    bias = bias_ref[...]
    for c in range(tq // CHUNK):
        q = q_ref[c * CHUNK:(c + 1) * CHUNK, :]
        start = i * tq + c * CHUNK
        ws = pl.ds(pl.multiple_of(start, CHUNK), BAND_KEYS)
        o_ref[c * CHUNK:(c + 1) * CHUNK, :] = _band_chunk(
            q, kpad[ws, :], vpad[ws, :], bias, start - WINDOW_B)


def _band_prompt(q3, kbvb, ext, *, tq):
    b, t, _ = q3.shape
    return pl.pallas_call(
        functools.partial(_band_kernel, tq=tq),
        grid=(b, H_B, t // tq),
        in_specs=[
            pl.BlockSpec((None, tq, LANES), lambda bi, h, i: (bi, i, QB_BLK + h)),
            pl.BlockSpec((None, t, LANES), lambda bi, h, i: (bi, 0, h)),
            pl.BlockSpec((None, t, LANES), lambda bi, h, i: (bi, 0, H_B + h)),
            pl.BlockSpec((None, 1, BIAS_EXT), lambda bi, h, i: (h, 0, 0)),
        ],
        out_specs=pl.BlockSpec((None, tq, LANES), lambda bi, h, i: (bi, i, h)),
        out_shape=jax.ShapeDtypeStruct((b, t, WB), BF16),
        scratch_shapes=[pltpu.VMEM((t + WINDOW_B, LANES), BF16),
                        pltpu.VMEM((t + WINDOW_B, LANES), BF16),
                        pltpu.VMEM((CHUNK, BAND_KEYS), F32)],
        compiler_params=_params("parallel", "parallel", "arbitrary"),
        name="band_prompt",
    )(q3, kbvb, kbvb, ext)


def _two_part_softmax(s_c, s_n, v_c, v_n):
    m = jnp.maximum(jnp.max(s_c, axis=1, keepdims=True), jnp.max(s_n, axis=1, keepdims=True))
    p_c = jnp.exp2(s_c - m)
    p_n = jnp.exp2(s_n - m)
    l = jnp.sum(p_c, axis=1, keepdims=True) + jnp.sum(p_n, axis=1, keepdims=True)
    acc = jnp.dot(p_c.astype(BF16), v_c, preferred_element_type=F32) + \
        jnp.dot(p_n.astype(BF16), v_n, preferred_element_type=F32)
    return acc, l


def _qk(q, k):
    return lax.dot_general(q, k, (((1,), (1,)), ((), ())), preferred_element_type=F32)


def _fox_sample_kernel(q_ref, kn_ref, vn_ref, kc_ref, vc_ref, crow_ref, ccol_ref, o_ref, *, past):
    q = q_ref[...]
    n = q.shape[0]
    cq = ccol_ref[...] * LOG2E
    s_c = _qk(q, kc_ref[...].astype(BF16)) + cq - crow_ref[:, :past] * LOG2E
    s_n = _qk(q, kn_ref[...].astype(BF16)) + cq - crow_ref[:, past:past + n] * LOG2E
    row = lax.broadcasted_iota(jnp.int32, (n, n), 0)
    col = lax.broadcasted_iota(jnp.int32, (n, n), 1)
    s_n = jnp.where(col <= row, s_n, NEG)
    acc, l = _two_part_softmax(s_c, s_n, vc_ref[...].astype(BF16), vn_ref[...].astype(BF16))
    o_ref[...] = (acc / l).astype(BF16)


def _new_kv(n, layer):
    return pl.BlockSpec((None, None, n, LANES), lambda bi, h: (layer, bi, 0, h))


def _fox_sample(q3, kn, vn, ck, cv, crow, ccol, *, layer):
    b, n, _ = q3.shape
    past = ck.shape[1]
    width = crow.shape[-1]
    return pl.pallas_call(
        functools.partial(_fox_sample_kernel, past=past),
        grid=(b, H_A),
        in_specs=[
            pl.BlockSpec((None, n, LANES), lambda bi, h: (bi, 0, h)),
            _new_kv(n, layer), _new_kv(n, layer),
            pl.BlockSpec((None, past, LANES), lambda bi, h: (bi, 0, h)),
            pl.BlockSpec((None, past, LANES), lambda bi, h: (bi, 0, h)),
            pl.BlockSpec((None, None, 1, width), lambda bi, h: (bi, h, 0, 0)),
            pl.BlockSpec((None, None, n, 1), lambda bi, h: (bi, h, 0, 0)),
        ],
        out_specs=pl.BlockSpec((None, n, LANES), lambda bi, h: (bi, 0, h)),
        out_shape=jax.ShapeDtypeStruct((b, n, WA), BF16),
        compiler_params=_params("parallel", "parallel"),
        name="fox_sample",
    )(q3, kn, vn, ck, cv, crow, ccol)


def _band_sample_kernel(q_ref, kn_ref, vn_ref, kc_ref, vc_ref, ext_ref, o_ref, *, first_key_pos):
    k = jnp.concatenate([kc_ref[...], kn_ref[...]], axis=0).astype(BF16)
    v = jnp.concatenate([vc_ref[...], vn_ref[...]], axis=0).astype(BF16)
    o_ref[...] = _band_chunk(q_ref[...], k, v, _band_bias_tile(ext_ref), first_key_pos)


def _band_sample(q3, kbvb, ck, cv, ext, *, first_key_pos):
    b, n, _ = q3.shape
    rows = ck.shape[1]
    return pl.pallas_call(
        functools.partial(_band_sample_kernel, first_key_pos=first_key_pos),
        grid=(b, H_B),
        in_specs=[
            pl.BlockSpec((None, n, LANES), lambda bi, h: (bi, 0, QB_BLK + h)),
            pl.BlockSpec((None, n, LANES), lambda bi, h: (bi, 0, h)),
            pl.BlockSpec((None, n, LANES), lambda bi, h: (bi, 0, H_B + h)),
            pl.BlockSpec((None, rows, LANES), lambda bi, h: (bi, 0, h)),
            pl.BlockSpec((None, rows, LANES), lambda bi, h: (bi, 0, h)),
            pl.BlockSpec((None, 1, BIAS_EXT), lambda bi, h: (h, 0, 0)),
        ],
        out_specs=pl.BlockSpec((None, n, LANES), lambda bi, h: (bi, 0, h)),
        out_shape=jax.ShapeDtypeStruct((b, n, WB), BF16),
        compiler_params=_params("parallel", "parallel"),
        name="band_sample",
    )(q3, kbvb, kbvb, ck, cv, ext)


def _diff_sample_kernel(q_ref, kn_ref, vn_ref, kc_ref, vc_ref, lq1_ref, lk1_ref, lq2_ref, lk2_ref,
                        g_ref, o_ref, *, lam_init):
    n = q_ref.shape[0]
    q2 = _stack_maps(q_ref[...])
    s_c = _qk(q2, kc_ref[...].astype(BF16))
    s_n = _qk(q2, kn_ref[...].astype(BF16))
    acc, l = _two_part_softmax(s_c, s_n, vc_ref[...].astype(BF16), vn_ref[...].astype(BF16))
    lam = _lambda(lq1_ref, lk1_ref, lq2_ref, lk2_ref, lam_init)
    o = acc / l
    o_ref[...] = _subln(o[:n] - lam * o[n:], g_ref[...], lam_init)


def _diff_sample(q3, kn, vn, ck, cv, lams, g_subln, *, layer, lam_init):
    b, n, _ = q3.shape
    past = ck.shape[1]
    return pl.pallas_call(
        functools.partial(_diff_sample_kernel, lam_init=lam_init),
        grid=(b, H_C),
        in_specs=[
            pl.BlockSpec((None, n, LANES), lambda bi, h: (bi, 0, QC_BLK + h)),
            _new_kv(n, layer), _new_kv(n, layer),
            pl.BlockSpec((None, past, LANES), lambda bi, h: (bi, 0, h)),
            pl.BlockSpec((None, past, LANES), lambda bi, h: (bi, 0, h)),
        ] + _lam_specs(),
        out_specs=pl.BlockSpec((None, n, LANES), lambda bi, h: (bi, 0, h)),
        out_shape=jax.ShapeDtypeStruct((b, n, WC), BF16),
        compiler_params=_params("parallel", "parallel"),
        name="diff_sample",
    )(q3, kn, vn, ck, cv, *lams, g_subln)


def _merge_kernel(x_ref, oa_ref, ob_ref, oc_ref, w_ref, o_ref):
    o = jnp.concatenate([oa_ref[...], ob_ref[...], oc_ref[...]], axis=1)
    o_ref[...] = x_ref[...] + jnp.dot(o, w_ref[...], preferred_element_type=F32)


def _merge(x, oa, ob, oc, w, *, tm):
    n, d = x.shape
    mix = w.shape[0]
    return pl.pallas_call(
        _merge_kernel,
        grid=(n // tm,),
        in_specs=[
            pl.BlockSpec((tm, d), lambda i: (i, 0)),
            pl.BlockSpec((tm, oa.shape[1]), lambda i: (i, 0)),
            pl.BlockSpec((tm, ob.shape[1]), lambda i: (i, 0)),
            pl.BlockSpec((tm, oc.shape[1]), lambda i: (i, 0)),
            pl.BlockSpec((mix, d), lambda i: (0, 0)),
        ],
        out_specs=pl.BlockSpec((tm, d), lambda i: (i, 0)),
        out_shape=jax.ShapeDtypeStruct((n, d), F32),
        compiler_params=_params("parallel"),
        name="merge",
    )(x, oa, ob, oc, w)


def _mlp_kernel(x_ref, g_ref, wu_ref, wd_ref, o_ref, h_ref):
    @pl.when(pl.program_id(1) == 0)
    def _():
        x = x_ref[...]
        h_ref[...] = _rms_rows(x, g_ref[...]).astype(BF16)
        o_ref[...] = x

    u = jnp.dot(h_ref[...], wu_ref[...], preferred_element_type=F32)
    a = jnp.square(jnp.maximum(u, 0.0)).astype(BF16)
    o_ref[...] += jnp.dot(a, wd_ref[...], preferred_element_type=F32)


def _mlp(x, g, wu, wd, *, tm, tf):
    n, d = x.shape
    ff = wu.shape[1]
    return pl.pallas_call(
        _mlp_kernel,
        grid=(n // tm, ff // tf),
        in_specs=[
            pl.BlockSpec((tm, d), lambda i, f: (i, 0)),
            pl.BlockSpec((1, d), lambda i, f: (0, 0)),
            pl.BlockSpec((d, tf), lambda i, f: (0, f)),
            pl.BlockSpec((tf, d), lambda i, f: (f, 0)),
        ],
        out_specs=pl.BlockSpec((tm, d), lambda i, f: (i, 0)),
        out_shape=jax.ShapeDtypeStruct((n, d), F32),
        scratch_shapes=[pltpu.VMEM((tm, d), BF16)],
        compiler_params=_params("parallel", "arbitrary"),
        name="mlp",
    )(x, g, wu, wd)


def _split_cols(w):
    sizes = [WA] * 3 + [H_A] + [WB] * 3 + [WC] * 3
    out, c = [], 0
    for s in sizes:
        out.append(w[..., c:c + s])
        c += s
    return out


def _rope_tables(pos):
    half = DIFF_DIM // 2
    inv = ROPE_THETA ** (-jnp.arange(half, dtype=F32) * 2.0 / DIFF_DIM)
    ang = pos.astype(F32)[:, None] * inv[None, :]
    cos, sin = jnp.cos(ang), jnp.sin(ang)
    cos_t = jnp.tile(cos, (1, LANES // half))
    sin_t = jnp.tile(jnp.concatenate([-sin, sin], axis=1), (1, LANES // DIFF_DIM))
    return cos_t, sin_t


def _band_bias_ext(rel_table):
    far = WINDOW_B - REL_MAX_PAST + (CHUNK - 1)
    tab = rel_table.astype(F32) * LOG2E
    ext = jnp.concatenate(
        [jnp.broadcast_to(tab[:, -1:], (tab.shape[0], far)), tab[:, ::-1],
         jnp.broadcast_to(tab[:, :1], (tab.shape[0], BIAS_EXT - far - tab.shape[1]))], axis=1)
    return ext.reshape(tab.shape[0], 1, BIAS_EXT)


def _block_diag_ones(block):
    r = jnp.arange(MXU_DIM)
    return (r[:, None] // block == r[None, :] // block).astype(BF16)


def _pick(n, pref):
    return pref if n % pref == 0 else n


def kernel(x_prompt, x_sample, cache_a_k, cache_a_v, cache_a_logf, cache_b_k, cache_b_v, cache_c_k, cache_c_v, w_in, b_f, g_qa, g_ka, g_qb, g_kb, rel_bias, g_qc, g_kc, lam_q1, lam_k1, lam_q2, lam_k2, g_subln, w_out, g_mix, g_mlp, w_up, w_down):
    depth = w_in.shape[0]
    bp, t, d = x_prompt.shape
    bs, ns, _ = x_sample.shape
    past = cache_a_k.shape[2]
    b_rows = cache_b_k.shape[2]
    keep_p = min(WINDOW_B, t)
    assert ns == CHUNK and b_rows == WINDOW_B and past % CHUNK == 0 and t % WINDOW_B == 0
    assert rel_bias.shape[-1] == REL_MAX_PAST + CHUNK
    n_p, n_s = bp * t, bs * ns

    tm_p = _pick(n_p, 512)
    tm_s = _pick(n_s, 512)
    tq = _pick(t, 512)

    cos_p, sin_p = _rope_tables(jnp.arange(t))
    cos_s, sin_s = _rope_tables(past + jnp.arange(ns))
    cos_s = jnp.tile(cos_s, (tm_s // ns, 1))
    sin_s = jnp.tile(sin_s, (tm_s // ns, 1))
    c_width = -(-(past + ns) // LANES) * LANES
    g128 = _block_diag_ones(HEAD_DIM)
    g64 = _block_diag_ones(DIFF_DIM)

    xp = x_prompt.reshape(n_p, d)
    xs = x_sample.reshape(n_s, d)
    bufs_p = bufs_s = None
    pbk, pbv, sbk, sbv = [], [], [], []
    ones = jnp.ones((HEAD_DIM,), F32)

    for l in range(depth):
        lam_init = 0.8 - 0.6 * math.exp(-0.3 * l)
        qa, ka, va, fa, qb, kb, vb, qc, kc, vc = _split_cols(w_in[l])
        gqc = jnp.tile(g_qc[l], 2)
        gkc = jnp.tile(g_kc[l], 2)
        q_scale = HEAD_DIM ** -0.5 * LOG2E
        prm = dict(
            w=jnp.concatenate([qa, qb, qc, ka, va, kb, vb, kc, vc], axis=1).astype(BF16),
            wf=jnp.pad(fa, ((0, 0), (0, LANES - H_A))).astype(BF16),
            bf=jnp.pad(b_f[l], (0, LANES - H_A)).reshape(1, LANES),
            gain=jnp.concatenate(
                [jnp.tile(g_qa[l] * q_scale, H_A), jnp.tile(g_qb[l] * q_scale, H_B),
                 jnp.tile(gqc * (DIFF_DIM ** -0.5 * LOG2E), H_C),
                 jnp.tile(g_ka[l], H_A), jnp.tile(ones, H_A),
                 jnp.tile(g_kb[l], H_B), jnp.tile(ones, H_B),
                 jnp.tile(gkc, H_C), jnp.tile(ones, H_C)]).reshape(1, P_WIDTH),
            g128=g128, g64=g64)
        gmix = g_mix[l].reshape(1, d)
        gmlp = g_mlp[l].reshape(1, d)
        gsub = g_subln[l].reshape(1, HEAD_DIM)
        lams = [a[l].reshape(1, DIFF_DIM) for a in (lam_q1, lam_k1, lam_q2, lam_k2)]
        wo = w_out[l].astype(BF16)
        wu = w_up[l].astype(BF16)
        wd = w_down[l].astype(BF16)
        ext = _band_bias_ext(rel_bias[l])

        qs, kbvb, bufs_p = _project(xp, gmix, prm, cos_p, sin_p, bufs_p, tm=tm_p, layer=l, depth=depth)
        ka_all, va_all, kc_all, vc_all, lf_all = bufs_p
        q3 = qs.reshape(bp, t, Q_WIDTH)
        kbvb3 = kbvb.reshape(bp, t, 2 * WB)
        logf_rows = jnp.transpose(lf_all[l].reshape(bp, t, H_A), (0, 2, 1)).reshape(bp * H_A, t)
        c_p = _cumsum_lanes(logf_rows)
        oa = _fox_prompt(q3, ka_all.reshape(depth, bp, t, WA), va_all.reshape(depth, bp, t, WA), c_p,
                         layer=l, tq=tq)
        ob = _band_prompt(q3, kbvb3, ext, tq=WINDOW_B)
        oc = _diff_prompt(q3, kc_all.reshape(depth, bp, t, WC), vc_all.reshape(depth, bp, t, WC),
                          lams, gsub, layer=l, tq=tq, lam_init=lam_init)
        xp = _merge(xp, oa.reshape(n_p, -1), ob.reshape(n_p, -1), oc.reshape(n_p, -1), wo, tm=_pick(n_p, 512))
        xp = _mlp(xp, gmlp, wu, wd, tm=_pick(n_p, 512), tf=1024)
        pbk.append(kbvb3[:, t - keep_p:, :WB].reshape(bp, keep_p, H_B, HEAD_DIM))
        pbv.append(kbvb3[:, t - keep_p:, WB:].reshape(bp, keep_p, H_B, HEAD_DIM))

        qs, kbvb, bufs_s = _project(xs, gmix, prm, cos_s, sin_s, bufs_s, tm=tm_s, layer=l, depth=depth)
        ka_all, va_all, kc_all, vc_all, lf_all = bufs_s
        q3 = qs.reshape(bs, ns, Q_WIDTH)
        kbvb3 = kbvb.reshape(bs, ns, 2 * WB)
        la = lf_all[l].reshape(bs, ns, H_A)
        logf_all = jnp.concatenate(
            [cache_a_logf[l].astype(F32), la, jnp.zeros((bs, c_width - past - ns, H_A), F32)], axis=1)
        c_s = _cumsum_lanes(jnp.transpose(logf_all, (0, 2, 1)).reshape(bs * H_A, c_width))
        crow = c_s.reshape(bs, H_A, 1, c_width)
        ccol = c_s[:, past:past + ns].reshape(bs, H_A, ns, 1)
        oa = _fox_sample(q3, ka_all.reshape(depth, bs, ns, WA), va_all.reshape(depth, bs, ns, WA),
                         cache_a_k[l].reshape(bs, past, WA), cache_a_v[l].reshape(bs, past, WA),
                         crow, ccol, layer=l)
        ob = _band_sample(q3, kbvb3, cache_b_k[l].reshape(bs, b_rows, WB), cache_b_v[l].reshape(bs, b_rows, WB),
                          ext, first_key_pos=past - b_rows)
        oc = _diff_sample(q3, kc_all.reshape(depth, bs, ns, WC), vc_all.reshape(depth, bs, ns, WC),
                          cache_c_k[l].reshape(bs, past, WC), cache_c_v[l].reshape(bs, past, WC),
                          lams, gsub, layer=l, lam_init=lam_init)
        xs = _merge(xs, oa.reshape(n_s, -1), ob.reshape(n_s, -1), oc.reshape(n_s, -1), wo, tm=_pick(n_s, 512))
        xs = _mlp(xs, gmlp, wu, wd, tm=_pick(n_s, 512), tf=1024)
        kb_new = kbvb3[:, :, :WB].reshape(bs, ns, H_B, HEAD_DIM)
        vb_new = kbvb3[:, :, WB:].reshape(bs, ns, H_B, HEAD_DIM)
        sbk.append(jnp.concatenate([cache_b_k[l][:, ns:], kb_new], axis=1))
        sbv.append(jnp.concatenate([cache_b_v[l][:, ns:], vb_new], axis=1))

    pak, pav, pck, pcv, pal = bufs_p
    sak, sav, sck, scv, sal = bufs_s
    return (xp.reshape(bp, t, d), xs.reshape(bs, ns, d),
            pak.reshape(depth, bp, t, H_A, HEAD_DIM), pav.reshape(depth, bp, t, H_A, HEAD_DIM),
            pal.reshape(depth, bp, t, H_A), jnp.stack(pbk), jnp.stack(pbv),
            pck.reshape(depth, bp, t, H_C, 2, DIFF_DIM), pcv.reshape(depth, bp, t, H_C, HEAD_DIM),
            sak.reshape(depth, bs, ns, H_A, HEAD_DIM), sav.reshape(depth, bs, ns, H_A, HEAD_DIM),
            sal.reshape(depth, bs, ns, H_A), jnp.stack(sbk), jnp.stack(sbv),
            sck.reshape(depth, bs, ns, H_C, 2, DIFF_DIM), scv.reshape(depth, bs, ns, H_C, HEAD_DIM))
```

```python
import functools
import math

import jax
import jax.numpy as jnp
from jax import lax
from jax.experimental import pallas as pl
from jax.experimental.pallas import tpu as pltpu

F32 = jnp.float32
BF16 = jnp.bfloat16

CHUNK = 64
HEAD_DIM = 128
H_A = 8
H_B = 4
H_C = 4
DIFF_DIM = HEAD_DIM // 2
BAND_CHUNKS = 8
WINDOW_B = BAND_CHUNKS * CHUNK
BAND_KEYS = WINDOW_B + CHUNK
REL_MAX_PAST = 128
ROPE_THETA = 10000.0
EPS = 1e-6
NEG = -1e30
LOG2E = math.log2(math.e)

LANES = 128
MXU_DIM = 256
VMEM_LIMIT = 52 * 1024 * 1024

WA, WB, WC = H_A * HEAD_DIM, H_B * HEAD_DIM, H_C * HEAD_DIM
Q_WIDTH = WA + WB + WC
QB_BLK = H_A
QC_BLK = H_A + H_B
P_WIDTH = 3 * (WA + WB + WC)
TN = 1024


def _params(*sem):
    return pltpu.CompilerParams(dimension_semantics=sem, vmem_limit_bytes=VMEM_LIMIT)


def _rms_rows(x, g):
    ms = jnp.mean(x * x, axis=-1, keepdims=True)
    return x * lax.rsqrt(ms + EPS) * g


def _log_sigmoid(z):
    return jnp.minimum(z, 0.0) - jnp.log1p(jnp.exp(-jnp.abs(z)))


ROW_CHUNK = 256


def _proj_kernel(*refs, tm, layer, n_alias, kc_transposed):
    (x_ref, g_ref, w_ref, wf_ref, bf_ref, gain_ref, cos_ref, sin_ref, g128_ref, g64_ref) = refs[:10]
    (q_ref, kbvb_ref, ka_ref, va_ref, kc_ref, vc_ref, lf_ref, h_ref, acc_ref) = refs[10 + n_alias:]
    j = pl.program_id(1)

    @pl.when(j == 0)
    def _():
        h = _rms_rows(x_ref[...], g_ref[...]).astype(BF16)
        h_ref[...] = h
        f = jnp.dot(h, wf_ref[...], preferred_element_type=F32) + bf_ref[...]
        lf_ref[...] = _log_sigmoid(f)[:, :H_A]

    acc_ref[...] = jnp.dot(h_ref[...], w_ref[...], preferred_element_type=F32)

    def finish(c0, c1, mode, dst_ref, d0, transposed=False):
        for r0 in range(0, tm, ROW_CHUNK):
            rs = slice(r0, r0 + ROW_CHUNK)
            for c in range(c0, c1, MXU_DIM):
                cs = slice(c, c + MXU_DIM)
                x = acc_ref[rs, cs]
                if mode == "id":
                    y = x
                else:
                    gm, hd = (g128_ref, HEAD_DIM) if mode == "n128" else (g64_ref, DIFF_DIM)
                    ssq = jnp.dot((x * x).astype(BF16), gm[...], preferred_element_type=F32)
                    y = x * lax.rsqrt(ssq * (1.0 / hd) + EPS) * gain_ref[:, cs]
                if mode == "n64r":
                    lane = lax.broadcasted_iota(jnp.int32, y.shape, 1)
                    first = (lane % DIFF_DIM) < (DIFF_DIM // 2)
                    cos = jnp.concatenate([cos_ref[rs, :]] * (MXU_DIM // LANES), axis=1)
                    sin = jnp.concatenate([sin_ref[rs, :]] * (MXU_DIM // LANES), axis=1)
                    partner = jnp.where(first, pltpu.roll(y, MXU_DIM - DIFF_DIM // 2, 1),
                                        pltpu.roll(y, DIFF_DIM // 2, 1))
                    y = y * cos + partner * sin
                ds = slice(d0 + c - c0, d0 + c - c0 + MXU_DIM)
                if transposed:
                    dst_ref[ds, rs] = y.T.astype(dst_ref.dtype)
                else:
                    dst_ref[rs, ds] = y.astype(dst_ref.dtype)

    plan = (
        ((0, WA, "n128", q_ref, 0),),
        ((0, WB, "n128", q_ref, WA), (WB, WB + WC, "n64r", q_ref, WA + WB)),
        ((0, WA, "n128", ka_ref, 0),),
        ((0, WA, "id", va_ref, 0),),
        ((0, WB, "n128", kbvb_ref, 0), (WB, 2 * WB, "id", kbvb_ref, WB)),
        ((0, WC, "n64r", kc_ref, 0, kc_transposed), (WC, 2 * WC, "id", vc_ref, 0)),
    )
    for jj, pieces in enumerate(plan):
        @pl.when(j == jj)
        def _(pieces=pieces):
            for piece in pieces:
                finish(*piece)


def _project(x, g_mix, prm, cos_t, sin_t, bufs, *, tm, layer, depth, kc_rows=None):
    n, d = x.shape
    period = cos_t.shape[0] // tm
    n_alias = 0 if bufs is None else len(bufs)
    stack = lambda width: jax.ShapeDtypeStruct((depth, n, width), F32)
    lay = lambda width: pl.BlockSpec((None, tm, width), lambda i, j: (layer, i, 0))
    const = lambda shape: pl.BlockSpec(shape, lambda i, j: (0,) * len(shape))
    if kc_rows is None:
        kc_shape, kc_spec = stack(WC), lay(WC)
    else:
        per = kc_rows // tm
        kc_shape = jax.ShapeDtypeStruct((depth, n // kc_rows, WC, kc_rows), F32)
        kc_spec = pl.BlockSpec((None, None, WC, tm), lambda i, j: (layer, i // per, 0, i % per))
    in_specs = [
        pl.BlockSpec((tm, d), lambda i, j: (i, 0)),
        const((1, d)),
        pl.BlockSpec((d, TN), lambda i, j: (0, j)),
        const((d, LANES)),
        const((1, LANES)),
        pl.BlockSpec((1, TN), lambda i, j: (0, j)),
        pl.BlockSpec((tm, LANES), lambda i, j: (i % period, 0)),
        pl.BlockSpec((tm, LANES), lambda i, j: (i % period, 0)),
        const((MXU_DIM, MXU_DIM)),
        const((MXU_DIM, MXU_DIM)),
    ] + [pl.BlockSpec(memory_space=pl.ANY)] * n_alias
    out = pl.pallas_call(
        functools.partial(_proj_kernel, tm=tm, layer=layer, n_alias=n_alias, kc_transposed=kc_rows is not None),
        grid=(n // tm, P_WIDTH // TN),
        in_specs=in_specs,
        out_specs=[
            pl.BlockSpec((tm, Q_WIDTH), lambda i, j: (i, 0)),
            pl.BlockSpec((tm, 2 * WB), lambda i, j: (i, 0)),
            lay(WA), lay(WA), kc_spec, lay(WC), lay(H_A),
        ],
        out_shape=[
            jax.ShapeDtypeStruct((n, Q_WIDTH), BF16),
            jax.ShapeDtypeStruct((n, 2 * WB), F32),
            stack(WA), stack(WA), kc_shape, stack(WC), stack(H_A),
        ],
        input_output_aliases={10 + k: 2 + k for k in range(n_alias)},
        scratch_shapes=[pltpu.VMEM((tm, d), BF16), pltpu.VMEM((tm, TN), F32)],
        compiler_params=_params("parallel", "arbitrary"),
        name="proj",
    )(x, g_mix, prm["w"], prm["wf"], prm["bf"], prm["gain"], cos_t, sin_t, prm["g128"], prm["g64"],
      *(bufs or ()))
    return out[0], out[1], tuple(out[2:])


def _cumsum_kernel(x_ref, o_ref):
    x = x_ref[...]
    n = x.shape[1]
    lane = lax.broadcasted_iota(jnp.int32, x.shape, 1)
    s = 1
    while s < n:
        x = x + jnp.where(lane >= s, pltpu.roll(x, s, 1), 0.0)
        s *= 2
    o_ref[...] = x


def _cumsum_lanes(x):
    r, n = x.shape
    rb = 8
    return pl.pallas_call(
        _cumsum_kernel,
        grid=(r // rb,),
        in_specs=[pl.BlockSpec((rb, n), lambda i: (i, 0))],
        out_specs=pl.BlockSpec((rb, n), lambda i: (i, 0)),
        out_shape=jax.ShapeDtypeStruct((r, n), F32),
        compiler_params=_params("parallel"),
        name="cumsum",
    )(x)


V_ROWS = HEAD_DIM + 16


def _flash_t(q, kbf, vt, m_ref, acc_ref, s_ref, mask_fn, *, tk, n_full):
    m_ref[...] = jnp.full(m_ref.shape, -jnp.inf, F32)
    acc_ref[...] = jnp.zeros(acc_ref.shape, F32)

    def scores(kb, slot):
        ks = pl.ds(pl.multiple_of(kb * tk, tk), tk)
        s_ref[slot] = lax.dot_general(kbf[ks, :], q, (((1,), (1,)), ((), ())), preferred_element_type=F32)

    def softmax_pv(kb, slot, masked):
        st = s_ref[slot]
        if masked:
            st = mask_fn(st)
        m_prev = m_ref[...]
        m_new = jnp.maximum(m_prev, jnp.max(st, axis=0, keepdims=True))
        m_ref[...] = m_new
        pt = jnp.exp2(st - m_new).astype(BF16)
        alpha = jnp.exp2(m_prev - m_new)
        acc_ref[...] = alpha * acc_ref[...] + jnp.dot(vt[kb], pt, preferred_element_type=F32)

    scores(0, 0)

    def pair(jp, carry):
        scores(2 * jp + 1, 1)
        softmax_pv(2 * jp, 0, False)
        scores(2 * jp + 2, 0)
        softmax_pv(2 * jp + 1, 1, False)
        return carry

    lax.fori_loop(0, n_full // 2, pair, 0)
    odd = n_full % 2 == 1

    @pl.when(odd)
    def _():
        scores(n_full, 1)
        softmax_pv(n_full - 1, 0, False)
        softmax_pv(n_full, 1, True)

    @pl.when(jnp.logical_not(odd))
    def _():
        softmax_pv(n_full, 0, True)


def _fill_vt(vt, v_ref, tk):
    ones_row = jnp.where(lax.broadcasted_iota(jnp.int32, (V_ROWS - HEAD_DIM, tk), 0) == 0, 1.0, 0.0)
    for kb in range(vt.shape[0]):
        vt[kb, :HEAD_DIM, :] = v_ref[kb * tk:(kb + 1) * tk, :].T.astype(BF16)
        vt[kb, HEAD_DIM:, :] = ones_row.astype(BF16)


def _split3(c):
    hi = c.astype(BF16).astype(F32)
    r1 = c - hi
    mid = r1.astype(BF16).astype(F32)
    lo = (r1 - mid).astype(BF16).astype(F32)
    return hi, mid, lo


def _decay_lanes(c_rep, key_side):
    hi, mid, lo = _split3(c_rep)
    lane = lax.broadcasted_iota(jnp.int32, c_rep.shape, 1)
    if key_side:
        parts = (-hi, -mid, -lo, 1.0, 1.0, 1.0)
    else:
        parts = (1.0, 1.0, 1.0, hi, mid, lo)
    vals = jnp.zeros(c_rep.shape, F32)
    for idx, part in enumerate(parts):
        vals = jnp.where(lane == idx, part, vals)
    return vals.astype(BF16)


def _fox_kernel(q_ref, k_ref, v_ref, crow_ref, o_ref, kbf, vt, crep, m_ref, acc_ref, s_ref, *, tq):
    i = pl.program_id(2)

    @pl.when(i == 0)
    def _():
        kbf[:, :HEAD_DIM] = k_ref[...].astype(BF16)
        _fill_vt(vt, v_ref, tq)
        for kb in range(vt.shape[0]):
            rs = slice(kb * tq, (kb + 1) * tq)
            c_rep = jnp.broadcast_to(crow_ref[:, rs] * LOG2E, (LANES, tq)).T
            crep[rs, :] = c_rep
            kbf[rs, HEAD_DIM:] = _decay_lanes(c_rep, True)

    cq_rep = crep[pl.ds(pl.multiple_of(i * tq, tq), tq), :]
    q = jnp.concatenate([q_ref[...], _decay_lanes(cq_rep, False)], axis=1)

    def mask_fn(st):
        kpos = lax.broadcasted_iota(jnp.int32, st.shape, 0)
        qpos = lax.broadcasted_iota(jnp.int32, st.shape, 1)
        return jnp.where(kpos <= qpos, st, NEG)

    _flash_t(q, kbf, vt, m_ref, acc_ref, s_ref, mask_fn, tk=tq, n_full=i)
    acc = acc_ref[...]
    o_ref[...] = (acc[:HEAD_DIM] / acc[HEAD_DIM:HEAD_DIM + 1]).T.astype(BF16)


def _fox_prompt(q3, ka, va, c, *, layer, tq):
    b, t, _ = q3.shape
    crow = c.reshape(b, H_A, 1, t)
    kv = lambda: pl.BlockSpec((None, None, t, LANES), lambda bi, h, i: (layer, bi, 0, h))
    return pl.pallas_call(
        functools.partial(_fox_kernel, tq=tq),
        grid=(b, H_A, t // tq),
        in_specs=[
            pl.BlockSpec((None, tq, LANES), lambda bi, h, i: (bi, i, h)),
            kv(), kv(),
            pl.BlockSpec((None, None, 1, t), lambda bi, h, i: (bi, h, 0, 0)),
        ],
        out_specs=pl.BlockSpec((None, tq, LANES), lambda bi, h, i: (bi, i, h)),
        out_shape=jax.ShapeDtypeStruct((b, t, WA), BF16),
        scratch_shapes=[
            pltpu.VMEM((t, 2 * HEAD_DIM), BF16), pltpu.VMEM((t // tq, V_ROWS, tq), BF16),
            pltpu.VMEM((t, LANES), F32),
            pltpu.VMEM((1, tq), F32), pltpu.VMEM((V_ROWS, tq), F32), pltpu.VMEM((2, tq, tq), F32),
        ],
        compiler_params=_params("parallel", "parallel", "arbitrary"),
        name="fox_prompt",
    )(q3, ka, va, crow)


def _lambda(lq1_ref, lk1_ref, lq2_ref, lk2_ref, lam_init):
    a = jnp.sum(lq1_ref[...] * lk1_ref[...], axis=-1, keepdims=True)
    b = jnp.sum(lq2_ref[...] * lk2_ref[...], axis=-1, keepdims=True)
    return jnp.exp(a) - jnp.exp(b) + lam_init


def _stack_maps(q):
    lane = lax.broadcasted_iota(jnp.int32, q.shape, 1)
    lo = lane < DIFF_DIM
    zero = jnp.zeros_like(q)
    return jnp.concatenate([jnp.where(lo, q, zero), jnp.where(lo, zero, q)], axis=0)


def _subln(o, g, lam_init):
    return (_rms_rows(o, g) * (1.0 - lam_init)).astype(BF16)


def _diff_kernel(q_ref, kt_ref, v_ref, lq1_ref, lk1_ref, lq2_ref, lk2_ref, g_ref, o_ref,
                 kbf, vt, m_ref, acc_ref, s_ref, *, tq, lam_init):
    i = pl.program_id(2)

    @pl.when(i == 0)
    def _():
        for kb in range(vt.shape[0]):
            rs = slice(kb * tq, (kb + 1) * tq)
            kbf[rs, :] = kt_ref[:, rs].T.astype(BF16)
        _fill_vt(vt, v_ref, tq)

    q2 = _stack_maps(q_ref[...])

    def mask_fn(st):
        kpos = lax.broadcasted_iota(jnp.int32, st.shape, 0)
        qpos = lax.broadcasted_iota(jnp.int32, st.shape, 1) % tq
        return jnp.where(kpos // CHUNK <= qpos // CHUNK, st, NEG)

    _flash_t(q2, kbf, vt, m_ref, acc_ref, s_ref, mask_fn, tk=tq, n_full=i)
    acc = acc_ref[...]
    o = acc[:HEAD_DIM] / acc[HEAD_DIM:HEAD_DIM + 1]
    lam = _lambda(lq1_ref, lk1_ref, lq2_ref, lk2_ref, lam_init)
    o_ref[...] = _subln((o[:, :tq] - lam * o[:, tq:]).T, g_ref[...], lam_init)


def _lam_specs():
    return [pl.BlockSpec((1, DIFF_DIM), lambda *a: (0, 0)) for _ in range(4)] + \
           [pl.BlockSpec((1, LANES), lambda *a: (0, 0))]


def _diff_prompt(q3, kct, vc, lams, g_subln, *, layer, tq, lam_init):
    b, t, _ = q3.shape
    return pl.pallas_call(
        functools.partial(_diff_kernel, tq=tq, lam_init=lam_init),
        grid=(b, H_C, t // tq),
        in_specs=[
            pl.BlockSpec((None, tq, LANES), lambda bi, h, i: (bi, i, QC_BLK + h)),
            pl.BlockSpec((None, None, HEAD_DIM, t), lambda bi, h, i: (layer, bi, h, 0)),
            pl.BlockSpec((None, None, t, LANES), lambda bi, h, i: (layer, bi, 0, h)),
        ] + _lam_specs(),
        out_specs=pl.BlockSpec((None, tq, LANES), lambda bi, h, i: (bi, i, h)),
        out_shape=jax.ShapeDtypeStruct((b, t, WC), BF16),
        scratch_shapes=[
            pltpu.VMEM((t, HEAD_DIM), BF16), pltpu.VMEM((t // tq, V_ROWS, tq), BF16),
            pltpu.VMEM((1, 2 * tq), F32), pltpu.VMEM((V_ROWS, 2 * tq), F32),
            pltpu.VMEM((2, tq, 2 * tq), F32),
        ],
        compiler_params=_params("parallel", "parallel", "arbitrary"),
        name="diff_prompt",
    )(q3, kct, vc, *lams, g_subln)


BIAS_EXT = 5 * LANES


def _band_bias_tile(ext):
    x = jnp.broadcast_to(ext, (CHUNK, BIAS_EXT))
    return pltpu.roll(x, BIAS_EXT - (CHUNK - 1), 1, stride=1, stride_axis=0)[:, :BAND_KEYS]


def _band_chunk(q, k, v, bias, first_key_pos):
    s = lax.dot_general(q, k, (((1,), (1,)), ((), ())), preferred_element_type=F32) + bias
    kpos = first_key_pos + lax.broadcasted_iota(jnp.int32, s.shape, 1)
    s = jnp.where(kpos >= 0, s, NEG)
    m = jnp.max(s, axis=1, keepdims=True)
    p = jnp.exp2(s - m)
    l = jnp.sum(p, axis=1, keepdims=True)
    return (jnp.dot(p.astype(BF16), v, preferred_element_type=F32) / l).astype(BF16)


BAND_STEP = WINDOW_B
BM_SPAN = 5 * LANES
BM_EXT = BM_SPAN + (2 * BAND_STEP // LANES - 1) * LANES


def _band_biasmask(g_ref, bm_ref):
    n_kb = 2 * BAND_STEP // LANES
    for kb in range(n_kb):
        start = LANES * (n_kb - 1 - kb)
        x = jnp.broadcast_to(g_ref[:, start:start + BM_SPAN], (LANES, BM_SPAN))
        tile = pltpu.roll(x, BM_SPAN - (LANES - 1), 1, stride=1, stride_axis=0)[:, :BAND_STEP]
        k = kb * LANES + lax.broadcasted_iota(jnp.int32, (LANES, BAND_STEP), 0)
        first = (lax.broadcasted_iota(jnp.int32, (LANES, BAND_STEP), 1) // CHUNK) * CHUNK
        tile = jnp.where(k >= first, tile, NEG)
        bm_ref[kb * LANES:(kb + 1) * LANES, :] = jnp.where(k < first + BAND_KEYS, tile, NEG)


def _band_kernel(*refs, t, n_alias):
    q_ref, k_ref, v_ref, g_ref = refs[:4]
    o_ref, pbk_ref, pbv_ref, kpad, vt, bm_ref = refs[4 + n_alias:]
    h = pl.program_id(1)
    i = pl.program_id(2)

    @pl.when(i == 0)
    def _():
        kpad[:BAND_STEP, :] = jnp.zeros((BAND_STEP, LANES), BF16)
        kpad[BAND_STEP:, :] = k_ref[...].astype(BF16)
        ones_row = jnp.where(lax.broadcasted_iota(jnp.int32, (V_ROWS - HEAD_DIM, BAND_STEP), 0) == 0, 1.0, 0.0)
        vt[0] = jnp.zeros((V_ROWS, BAND_STEP), BF16)
        for kb in range(t // BAND_STEP):
            vt[kb + 1, :HEAD_DIM, :] = v_ref[kb * BAND_STEP:(kb + 1) * BAND_STEP, :].T.astype(BF16)
            vt[kb + 1, HEAD_DIM:, :] = ones_row.astype(BF16)
        _band_biasmask(g_ref, bm_ref)
        for hh in range(H_B):
            @pl.when(h == hh)
            def _(hh=hh):
                pbk_ref[pl.ds(hh, WINDOW_B, stride=H_B), :] = k_ref[t - WINDOW_B:, :]
                pbv_ref[pl.ds(hh, WINDOW_B, stride=H_B), :] = v_ref[t - WINDOW_B:, :]

    ks = pl.ds(pl.multiple_of(i * BAND_STEP, BAND_STEP), 2 * BAND_STEP)
    st = lax.dot_general(kpad[ks, :], q_ref[...], (((1,), (1,)), ((), ())),
                         preferred_element_type=F32) + bm_ref[...]
    kpos = (i - 1) * BAND_STEP + lax.broadcasted_iota(jnp.int32, st.shape, 0)
    st = jnp.where(kpos >= 0, st, NEG)
    m = jnp.max(st, axis=0, keepdims=True)
    p = jnp.exp2(st - m).astype(BF16)
    acc = jnp.dot(vt[i], p[:BAND_STEP], preferred_element_type=F32) + \
        jnp.dot(vt[i + 1], p[BAND_STEP:], preferred_element_type=F32)
    o_ref[...] = (acc[:HEAD_DIM] / acc[HEAD_DIM:HEAD_DIM + 1]).T.astype(BF16)


def _band_prompt(q3, kbvb, g, bufs, *, layer, depth):
    b, t, _ = q3.shape
    n_alias = 0 if bufs is None else len(bufs)
    keep = jax.ShapeDtypeStruct((depth, b, WINDOW_B * H_B, HEAD_DIM), F32)
    keep_spec = pl.BlockSpec((None, None, WINDOW_B * H_B, HEAD_DIM), lambda bi, h, i: (layer, bi, 0, 0))
    out = pl.pallas_call(
        functools.partial(_band_kernel, t=t, n_alias=n_alias),
        grid=(b, H_B, t // BAND_STEP),
        in_specs=[
            pl.BlockSpec((None, BAND_STEP, LANES), lambda bi, h, i: (bi, i, QB_BLK + h)),
            pl.BlockSpec((None, t, LANES), lambda bi, h, i: (bi, 0, h)),
            pl.BlockSpec((None, t, LANES), lambda bi, h, i: (bi, 0, H_B + h)),
            pl.BlockSpec((None, 1, BM_EXT), lambda bi, h, i: (h, 0, 0)),
        ] + [pl.BlockSpec(memory_space=pl.ANY)] * n_alias,
        out_specs=[pl.BlockSpec((None, BAND_STEP, LANES), lambda bi, h, i: (bi, i, h)), keep_spec, keep_spec],
        out_shape=[jax.ShapeDtypeStruct((b, t, WB), BF16), keep, keep],
        input_output_aliases={4 + k: 1 + k for k in range(n_alias)},
        scratch_shapes=[pltpu.VMEM((t + BAND_STEP, LANES), BF16),
                        pltpu.VMEM((t // BAND_STEP + 1, V_ROWS, BAND_STEP), BF16),
                        pltpu.VMEM((2 * BAND_STEP, BAND_STEP), F32)],
        compiler_params=_params("parallel", "arbitrary", "arbitrary"),
        name="band_prompt",
    )(q3, kbvb, kbvb, g, *(bufs or ()))
    return out[0], tuple(out[1:])


SAMPLE_KV = 512


def _qk(q, k):
    return lax.dot_general(q, k, (((1,), (1,)), ((), ())), preferred_element_type=F32)


def _online_rows(s, v, m_ref, l_ref, acc_ref, h):
    m_prev = m_ref[h]
    m_new = jnp.maximum(m_prev, jnp.max(s, axis=1, keepdims=True))
    alpha = jnp.exp2(m_prev - m_new)
    p = jnp.exp2(s - m_new)
    l_ref[h] = alpha * l_ref[h] + jnp.sum(p, axis=1, keepdims=True)
    acc_ref[h] = alpha * acc_ref[h] + jnp.dot(p.astype(BF16), v, preferred_element_type=F32)
    m_ref[h] = m_new


def _reset_rows(m_ref, l_ref, acc_ref):
    m_ref[...] = jnp.full(m_ref.shape, -jnp.inf, F32)
    l_ref[...] = jnp.zeros(l_ref.shape, F32)
    acc_ref[...] = jnp.zeros(acc_ref.shape, F32)


def _fox_sample_kernel(q_ref, kn_ref, vn_ref, kc_ref, vc_ref, crow_ref, cnew_ref, ccol_ref, o_ref,
                       m_ref, l_ref, acc_ref):
    j = pl.program_id(1)
    n = q_ref.shape[0]

    @pl.when(j == 0)
    def _():
        _reset_rows(m_ref, l_ref, acc_ref)

    for h in range(H_A):
        hs = slice(h * HEAD_DIM, (h + 1) * HEAD_DIM)
        rows = pl.ds(h, SAMPLE_KV, stride=H_A)
        s = _qk(q_ref[:, hs], kc_ref[rows, :].astype(BF16)) + \
            (ccol_ref[h] - crow_ref[h:h + 1, :]) * LOG2E
        _online_rows(s, vc_ref[rows, :].astype(BF16), m_ref, l_ref, acc_ref, h)

    @pl.when(j == pl.num_programs(1) - 1)
    def _():
        row = lax.broadcasted_iota(jnp.int32, (n, n), 0)
        col = lax.broadcasted_iota(jnp.int32, (n, n), 1)
        for h in range(H_A):
            hs = slice(h * HEAD_DIM, (h + 1) * HEAD_DIM)
            s = _qk(q_ref[:, hs], kn_ref[:, hs].astype(BF16)) + \
                (ccol_ref[h] - cnew_ref[h:h + 1, :n]) * LOG2E
            _online_rows(jnp.where(col <= row, s, NEG), vn_ref[:, hs].astype(BF16), m_ref, l_ref, acc_ref, h)
            o_ref[:, hs] = (acc_ref[h] / l_ref[h]).astype(BF16)


def _sample_scratch(heads, rows):
    return [pltpu.VMEM((heads, rows, 1), F32), pltpu.VMEM((heads, rows, 1), F32),
            pltpu.VMEM((heads, rows, HEAD_DIM), F32)]


def _fox_sample(q3, kn, vn, ck, cv, c, *, layer):
    b, n, _ = q3.shape
    past = ck.shape[2] // H_A
    new = lambda: pl.BlockSpec((None, None, n, WA), lambda bi, j: (layer, bi, 0, 0))
    cache = lambda: pl.BlockSpec((None, None, SAMPLE_KV * H_A, HEAD_DIM), lambda bi, j: (layer, bi, j, 0))
    ccol = c[:, :, past:past + n].reshape(b, H_A, n, 1)
    return pl.pallas_call(
        _fox_sample_kernel,
        grid=(b, past // SAMPLE_KV),
        in_specs=[
            pl.BlockSpec((None, n, WA), lambda bi, j: (bi, 0, 0)),
            new(), new(), cache(), cache(),
            pl.BlockSpec((None, H_A, SAMPLE_KV), lambda bi, j: (bi, 0, j)),
            pl.BlockSpec((None, H_A, LANES), lambda bi, j: (bi, 0, past // LANES)),
            pl.BlockSpec((None, H_A, n, 1), lambda bi, j: (bi, 0, 0, 0)),
        ],
        out_specs=pl.BlockSpec((None, n, WA), lambda bi, j: (bi, 0, 0)),
        out_shape=jax.ShapeDtypeStruct((b, n, WA), BF16),
        scratch_shapes=_sample_scratch(H_A, n),
        compiler_params=_params("parallel", "arbitrary"),
        name="fox_sample",
    )(q3, kn, vn, ck, cv, c, c, ccol)


def _band_sample_kernel(*refs, first_key_pos, n_alias):
    q_ref, kbvb_ref, kc_ref, vc_ref, ext_ref = refs[:5]
    o_ref, sbk_ref, sbv_ref = refs[5 + n_alias:]
    n = q_ref.shape[0]
    rows = kc_ref.shape[0] // H_B
    keep = rows - n
    sbk_ref[:keep * H_B, :] = kc_ref[n * H_B:, :]
    sbv_ref[:keep * H_B, :] = vc_ref[n * H_B:, :]
    for h in range(H_B):
        hs = slice(h * HEAD_DIM, (h + 1) * HEAD_DIM)
        kn = kbvb_ref[:, hs]
        vn = kbvb_ref[:, WB + h * HEAD_DIM:WB + (h + 1) * HEAD_DIM]
        new_rows = pl.ds(keep * H_B + h, n, stride=H_B)
        sbk_ref[new_rows, :] = kn
        sbv_ref[new_rows, :] = vn
        cached = pl.ds(h, rows, stride=H_B)
        k = jnp.concatenate([kc_ref[cached, :], kn], axis=0).astype(BF16)
        v = jnp.concatenate([vc_ref[cached, :], vn], axis=0).astype(BF16)
        o_ref[:, hs] = _band_chunk(q_ref[:, hs], k, v, _band_bias_tile(ext_ref[h]), first_key_pos)


def _band_sample(q3, kbvb, ck, cv, ext, bufs, *, layer, first_key_pos):
    b, n, _ = q3.shape
    depth, _, rows_h = ck.shape[:3]
    n_alias = 0 if bufs is None else len(bufs)
    roll_spec = lambda: pl.BlockSpec((None, None, rows_h, HEAD_DIM), lambda bi: (layer, bi, 0, 0))
    rolled = jax.ShapeDtypeStruct((depth, b, rows_h, HEAD_DIM), F32)
    out = pl.pallas_call(
        functools.partial(_band_sample_kernel, first_key_pos=first_key_pos, n_alias=n_alias),
        grid=(b,),
        in_specs=[
            pl.BlockSpec((None, n, WB), lambda bi: (bi, 0, WA // WB)),
            pl.BlockSpec((None, n, 2 * WB), lambda bi: (bi, 0, 0)),
            roll_spec(), roll_spec(),
            pl.BlockSpec((H_B, 1, BIAS_EXT), lambda bi: (0, 0, 0)),
        ] + [pl.BlockSpec(memory_space=pl.ANY)] * n_alias,
        out_specs=[pl.BlockSpec((None, n, WB), lambda bi: (bi, 0, 0)), roll_spec(), roll_spec()],
        out_shape=[jax.ShapeDtypeStruct((b, n, WB), BF16), rolled, rolled],
        input_output_aliases={5 + k: 1 + k for k in range(n_alias)},
        compiler_params=_params("parallel"),
        name="band_sample",
    )(q3, kbvb, ck, cv, ext, *(bufs or ()))
    return out[0], tuple(out[1:])


def _diff_sample_kernel(q_ref, kn_ref, vn_ref, kt_ref, vc_ref, lq1_ref, lk1_ref, lq2_ref, lk2_ref,
                        g_ref, o_ref, m_ref, l_ref, acc_ref, *, lam_init):
    j = pl.program_id(1)
    n = q_ref.shape[0]

    @pl.when(j == 0)
    def _():
        _reset_rows(m_ref, l_ref, acc_ref)

    for h in range(H_C):
        q = q_ref[:, h * HEAD_DIM:(h + 1) * HEAD_DIM]
        s = jnp.concatenate(
            [jnp.dot(q[:, :DIFF_DIM], kt_ref[h, 0].astype(BF16), preferred_element_type=F32),
             jnp.dot(q[:, DIFF_DIM:], kt_ref[h, 1].astype(BF16), preferred_element_type=F32)], axis=0)
        _online_rows(s, vc_ref[pl.ds(h, SAMPLE_KV, stride=H_C), :].astype(BF16), m_ref, l_ref, acc_ref, h)

    @pl.when(j == pl.num_programs(1) - 1)
    def _():
        lam = _lambda(lq1_ref, lk1_ref, lq2_ref, lk2_ref, lam_init)
        for h in range(H_C):
            hs = slice(h * HEAD_DIM, (h + 1) * HEAD_DIM)
            s = _qk(_stack_maps(q_ref[:, hs]), kn_ref[:, hs].astype(BF16))
            _online_rows(s, vn_ref[:, hs].astype(BF16), m_ref, l_ref, acc_ref, h)
            o = acc_ref[h] / l_ref[h]
            o_ref[:, hs] = _subln(o[:n] - lam * o[n:], g_ref[...], lam_init)


def _diff_sample(q3, kn, vn, ckt, cv, lams, g_subln, *, layer, lam_init):
    b, n, _ = q3.shape
    past = cv.shape[2] // H_C
    new = lambda: pl.BlockSpec((None, None, n, WC), lambda bi, j: (layer, bi, 0, 0))
    return pl.pallas_call(
        functools.partial(_diff_sample_kernel, lam_init=lam_init),
        grid=(b, past // SAMPLE_KV),
        in_specs=[
            pl.BlockSpec((None, n, WC), lambda bi, j: (bi, 0, (WA + WB) // WC)),
            new(), new(),
            pl.BlockSpec((None, None, H_C, 2, DIFF_DIM, SAMPLE_KV), lambda bi, j: (layer, bi, 0, 0, 0, j)),
            pl.BlockSpec((None, None, SAMPLE_KV * H_C, HEAD_DIM), lambda bi, j: (layer, bi, j, 0)),
        ] + _lam_specs(),
        out_specs=pl.BlockSpec((None, n, WC), lambda bi, j: (bi, 0, 0)),
        out_shape=jax.ShapeDtypeStruct((b, n, WC), BF16),
        scratch_shapes=_sample_scratch(H_C, 2 * n),
        compiler_params=_params("parallel", "arbitrary"),
        name="diff_sample",
    )(q3, kn, vn, ckt, cv, *lams, g_subln)


def _merge_kernel(x_ref, oa_ref, ob_ref, oc_ref, w_ref, o_ref):
    o = jnp.concatenate([oa_ref[...], ob_ref[...], oc_ref[...]], axis=1)
    o_ref[...] = x_ref[...] + jnp.dot(o, w_ref[...], preferred_element_type=F32)


def _merge(x, oa, ob, oc, w, *, tm):
    n, d = x.shape
    mix = w.shape[0]
    return pl.pallas_call(
        _merge_kernel,
        grid=(n // tm,),
        in_specs=[
            pl.BlockSpec((tm, d), lambda i: (i, 0)),
            pl.BlockSpec((tm, oa.shape[1]), lambda i: (i, 0)),
            pl.BlockSpec((tm, ob.shape[1]), lambda i: (i, 0)),
            pl.BlockSpec((tm, oc.shape[1]), lambda i: (i, 0)),
            pl.BlockSpec((mix, d), lambda i: (0, 0)),
        ],
        out_specs=pl.BlockSpec((tm, d), lambda i: (i, 0)),
        out_shape=jax.ShapeDtypeStruct((n, d), F32),
        compiler_params=_params("parallel"),
        name="merge",
    )(x, oa, ob, oc, w)


def _mlp_kernel(x_ref, g_ref, wu_ref, wd_ref, o_ref, h_ref):
    @pl.when(pl.program_id(1) == 0)
    def _():
        x = x_ref[...]
        h_ref[...] = _rms_rows(x, g_ref[...]).astype(BF16)
        o_ref[...] = x

    u = jnp.dot(h_ref[...], wu_ref[...], preferred_element_type=F32)
    a = jnp.square(jnp.maximum(u, 0.0)).astype(BF16)
    o_ref[...] += jnp.dot(a, wd_ref[...], preferred_element_type=F32)


def _mlp(x, g, wu, wd, *, tm, tf):
    n, d = x.shape
    ff = wu.shape[1]
    return pl.pallas_call(
        _mlp_kernel,
        grid=(n // tm, ff // tf),
        in_specs=[
            pl.BlockSpec((tm, d), lambda i, f: (i, 0)),
            pl.BlockSpec((1, d), lambda i, f: (0, 0)),
            pl.BlockSpec((d, tf), lambda i, f: (0, f)),
            pl.BlockSpec((tf, d), lambda i, f: (f, 0)),
        ],
        out_specs=pl.BlockSpec((tm, d), lambda i, f: (i, 0)),
        out_shape=jax.ShapeDtypeStruct((n, d), F32),
        scratch_shapes=[pltpu.VMEM((tm, d), BF16)],
        compiler_params=_params("parallel", "arbitrary"),
        name="mlp",
    )(x, g, wu, wd)


def _split_cols(w):
    sizes = [WA] * 3 + [H_A] + [WB] * 3 + [WC] * 3
    out, c = [], 0
    for s in sizes:
        out.append(w[..., c:c + s])
        c += s
    return out


def _rope_tables(pos):
    half = DIFF_DIM // 2
    inv = ROPE_THETA ** (-jnp.arange(half, dtype=F32) * 2.0 / DIFF_DIM)
    ang = pos.astype(F32)[:, None] * inv[None, :]
    cos, sin = jnp.cos(ang), jnp.sin(ang)
    cos_t = jnp.tile(cos, (1, LANES // half))
    sin_t = jnp.tile(jnp.concatenate([-sin, sin], axis=1), (1, LANES // DIFF_DIM))
    return cos_t, sin_t


def _band_bias_ext(rel_table):
    far = WINDOW_B - REL_MAX_PAST + (CHUNK - 1)
    tab = rel_table.astype(F32) * LOG2E
    ext = jnp.concatenate(
        [jnp.broadcast_to(tab[:, -1:], (tab.shape[0], far)), tab[:, ::-1],
         jnp.broadcast_to(tab[:, :1], (tab.shape[0], BIAS_EXT - far - tab.shape[1]))], axis=1)
    return ext.reshape(tab.shape[0], 1, BIAS_EXT)


def _band_bias_reversed(ext):
    n_off = BAND_KEYS + CHUNK - 1
    top = CHUNK - 1 + LANES - 1 + BM_EXT - BM_SPAN
    rev = ext[:, :, :n_off][:, :, ::-1]
    lead = top - (n_off - 1)
    return jnp.pad(rev, ((0, 0), (0, 0), (lead, BM_EXT - lead - n_off)))


def _block_diag_ones(block):
    r = jnp.arange(MXU_DIM)
    return (r[:, None] // block == r[None, :] // block).astype(BF16)


def _pick(n, pref):
    return pref if n % pref == 0 else n


def kernel(x_prompt, x_sample, cache_a_k, cache_a_v, cache_a_logf, cache_b_k, cache_b_v, cache_c_k, cache_c_v, w_in, b_f, g_qa, g_ka, g_qb, g_kb, rel_bias, g_qc, g_kc, lam_q1, lam_k1, lam_q2, lam_k2, g_subln, w_out, g_mix, g_mlp, w_up, w_down):
    depth = w_in.shape[0]
    bp, t, d = x_prompt.shape
    bs, ns, _ = x_sample.shape
    past = cache_a_k.shape[2]
    b_rows = cache_b_k.shape[2]
    assert ns == CHUNK and b_rows == WINDOW_B and past % CHUNK == 0 and t % WINDOW_B == 0
    assert rel_bias.shape[-1] == REL_MAX_PAST + CHUNK
    n_p, n_s = bp * t, bs * ns

    tm_p = _pick(n_p, 512)
    tm_s = _pick(n_s, 512)
    tq = _pick(t, 512)

    cos_p, sin_p = _rope_tables(jnp.arange(t))
    cos_s, sin_s = _rope_tables(past + jnp.arange(ns))
    cos_s = jnp.tile(cos_s, (tm_s // ns, 1))
    sin_s = jnp.tile(sin_s, (tm_s // ns, 1))
    c_width = -(-(past + ns) // LANES) * LANES
    g128 = _block_diag_ones(HEAD_DIM)
    g64 = _block_diag_ones(DIFF_DIM)

    xp = x_prompt.reshape(n_p, d)
    xs = x_sample.reshape(n_s, d)
    bufs_p = bufs_s = band_p = band_s = None
    ones = jnp.ones((HEAD_DIM,), F32)
    cache_c_kt = jnp.transpose(cache_c_k, (0, 1, 3, 4, 5, 2))
    cache_logf_rows = jnp.transpose(cache_a_logf.astype(F32), (0, 1, 3, 2))
    frames_by_head = lambda a: a.reshape(a.shape[0], a.shape[1], a.shape[2] * a.shape[3], HEAD_DIM)
    cak, cav, cbk, cbv, ccv = map(frames_by_head, (cache_a_k, cache_a_v, cache_b_k, cache_b_v, cache_c_v))

    for l in range(depth):
        lam_init = 0.8 - 0.6 * math.exp(-0.3 * l)
        qa, ka, va, fa, qb, kb, vb, qc, kc, vc = _split_cols(w_in[l])
        gqc = jnp.tile(g_qc[l], 2)
        gkc = jnp.tile(g_kc[l], 2)
        q_scale = HEAD_DIM ** -0.5 * LOG2E
        prm = dict(
            w=jnp.concatenate([qa, qb, qc, ka, va, kb, vb, kc, vc], axis=1).astype(BF16),
            wf=jnp.pad(fa, ((0, 0), (0, LANES - H_A))).astype(BF16),
            bf=jnp.pad(b_f[l], (0, LANES - H_A)).reshape(1, LANES),
            gain=jnp.concatenate(
                [jnp.tile(g_qa[l] * q_scale, H_A), jnp.tile(g_qb[l] * q_scale, H_B),
                 jnp.tile(gqc * (DIFF_DIM ** -0.5 * LOG2E), H_C),
                 jnp.tile(g_ka[l], H_A), jnp.tile(ones, H_A),
                 jnp.tile(g_kb[l], H_B), jnp.tile(ones, H_B),
                 jnp.tile(gkc, H_C), jnp.tile(ones, H_C)]).reshape(1, P_WIDTH),
            g128=g128, g64=g64)
        gmix = g_mix[l].reshape(1, d)
        gmlp = g_mlp[l].reshape(1, d)
        gsub = g_subln[l].reshape(1, HEAD_DIM)
        lams = [a[l].reshape(1, DIFF_DIM) for a in (lam_q1, lam_k1, lam_q2, lam_k2)]
        wo = w_out[l].astype(BF16)
        wu = w_up[l].astype(BF16)
        wd = w_down[l].astype(BF16)
        ext = _band_bias_ext(rel_bias[l])
        ext_rev = _band_bias_reversed(ext)

        qs, kbvb, bufs_p = _project(xp, gmix, prm, cos_p, sin_p, bufs_p, tm=tm_p, layer=l, depth=depth,
                                    kc_rows=t)
        ka_all, va_all, kct_all, vc_all, lf_all = bufs_p
        q3 = qs.reshape(bp, t, Q_WIDTH)
        logf_rows = jnp.transpose(lf_all[l].reshape(bp, t, H_A), (0, 2, 1)).reshape(bp * H_A, t)
        c_p = _cumsum_lanes(logf_rows)
        oa = _fox_prompt(q3, ka_all.reshape(depth, bp, t, WA), va_all.reshape(depth, bp, t, WA), c_p,
                         layer=l, tq=tq)
        ob, band_p = _band_prompt(q3, kbvb.reshape(bp, t, 2 * WB), ext_rev, band_p, layer=l, depth=depth)
        oc = _diff_prompt(q3, kct_all, vc_all.reshape(depth, bp, t, WC),
                          lams, gsub, layer=l, tq=tq, lam_init=lam_init)
        xp = _merge(xp, oa.reshape(n_p, -1), ob.reshape(n_p, -1), oc.reshape(n_p, -1), wo, tm=_pick(n_p, 512))
        xp = _mlp(xp, gmlp, wu, wd, tm=_pick(n_p, 512), tf=1024)

        qs, kbvb, bufs_s = _project(xs, gmix, prm, cos_s, sin_s, bufs_s, tm=tm_s, layer=l, depth=depth)
        ka_all, va_all, kc_all, vc_all, lf_all = bufs_s
        q3 = qs.reshape(bs, ns, Q_WIDTH)
        la_rows = jnp.transpose(lf_all[l].reshape(bs, ns, H_A), (0, 2, 1))
        logf_all = jnp.concatenate(
            [cache_logf_rows[l], la_rows, jnp.zeros((bs, H_A, c_width - past - ns), F32)], axis=2)
        c_s = _cumsum_lanes(logf_all.reshape(bs * H_A, c_width)).reshape(bs, H_A, c_width)
        oa = _fox_sample(q3, ka_all.reshape(depth, bs, ns, WA), va_all.reshape(depth, bs, ns, WA),
                         cak, cav, c_s, layer=l)
        ob, band_s = _band_sample(q3, kbvb.reshape(bs, ns, 2 * WB), cbk, cbv, ext, band_s,
                                  layer=l, first_key_pos=past - b_rows)
        oc = _diff_sample(q3, kc_all.reshape(depth, bs, ns, WC), vc_all.reshape(depth, bs, ns, WC),
                          cache_c_kt, ccv, lams, gsub, layer=l, lam_init=lam_init)
        xs = _merge(xs, oa.reshape(n_s, -1), ob.reshape(n_s, -1), oc.reshape(n_s, -1), wo, tm=_pick(n_s, 512))
        xs = _mlp(xs, gmlp, wu, wd, tm=_pick(n_s, 512), tf=1024)

    pak, pav, pckt, pcv, pal = bufs_p
    sak, sav, sck, scv, sal = bufs_s
    pck = jnp.transpose(pckt.reshape(depth, bp, H_C, 2, DIFF_DIM, t), (0, 1, 5, 2, 3, 4))
    return (xp.reshape(bp, t, d), xs.reshape(bs, ns, d),
            pak.reshape(depth, bp, t, H_A, HEAD_DIM), pav.reshape(depth, bp, t, H_A, HEAD_DIM),
            pal.reshape(depth, bp, t, H_A),
            band_p[0].reshape(depth, bp, WINDOW_B, H_B, HEAD_DIM), band_p[1].reshape(depth, bp, WINDOW_B, H_B, HEAD_DIM),
            pck, pcv.reshape(depth, bp, t, H_C, HEAD_DIM),
            sak.reshape(depth, bs, ns, H_A, HEAD_DIM), sav.reshape(depth, bs, ns, H_A, HEAD_DIM),
            sal.reshape(depth, bs, ns, H_A),
            band_s[0].reshape(depth, bs, b_rows, H_B, HEAD_DIM), band_s[1].reshape(depth, bs, b_rows, H_B, HEAD_DIM),
            sck.reshape(depth, bs, ns, H_C, 2, DIFF_DIM), scv.reshape(depth, bs, ns, H_C, HEAD_DIM))
```

```python
import functools
import math

import jax
import jax.numpy as jnp
from jax import lax
from jax.experimental import pallas as pl
from jax.experimental.pallas import tpu as pltpu

F32 = jnp.float32
BF16 = jnp.bfloat16

CHUNK = 64
HEAD_DIM = 128
H_A = 8
H_B = 4
H_C = 4
DIFF_DIM = HEAD_DIM // 2
BAND_CHUNKS = 8
WINDOW_B = BAND_CHUNKS * CHUNK
BAND_KEYS = WINDOW_B + CHUNK
REL_MAX_PAST = 128
ROPE_THETA = 10000.0
EPS = 1e-6
NEG = -1e30
LOG2E = math.log2(math.e)

LANES = 128
MXU_DIM = 256
VMEM_LIMIT = 52 * 1024 * 1024

WA, WB, WC = H_A * HEAD_DIM, H_B * HEAD_DIM, H_C * HEAD_DIM
Q_WIDTH = WA + WB + WC
QB_BLK = H_A
QC_BLK = H_A + H_B
P_WIDTH = 3 * (WA + WB + WC)
TN = 1024


def _params(*sem):
    return pltpu.CompilerParams(dimension_semantics=sem, vmem_limit_bytes=VMEM_LIMIT)


def _rms_rows(x, g):
    ms = jnp.mean(x * x, axis=-1, keepdims=True)
    return x * lax.rsqrt(ms + EPS) * g


def _log_sigmoid(z):
    return jnp.minimum(z, 0.0) - jnp.log1p(jnp.exp(-jnp.abs(z)))


ROW_CHUNK = 256


N_ALIAS = 5


def _proj_kernel(*refs, tm, kc_transposed):
    (x_ref, g_ref, w_ref, wf_ref, bf_ref, gain_ref, cos_ref, sin_ref, g128_ref, g64_ref) = refs[:10]
    (q_ref, kbvb_ref, ka_ref, va_ref, kc_ref, vc_ref, lf_ref, h_ref, acc_ref) = refs[10 + N_ALIAS:]
    j = pl.program_id(1)

    @pl.when(j == 0)
    def _():
        h = _rms_rows(x_ref[...], g_ref[...]).astype(BF16)
        h_ref[...] = h
        f = jnp.dot(h, wf_ref[...], preferred_element_type=F32) + bf_ref[...]
        lf_ref[...] = _log_sigmoid(f)[:, :H_A]

    def finish(tile, c0, c1, mode, dst_ref, d0, transposed=False):
        acc = acc_ref.at[tile % 2]
        for r0 in range(0, tm, ROW_CHUNK):
            rs = slice(r0, r0 + ROW_CHUNK)
            for c in range(c0, c1, MXU_DIM):
                cs = slice(c, c + MXU_DIM)
                x = acc[rs, cs]
                if mode == "id":
                    y = x
                else:
                    gm, hd = (g128_ref, HEAD_DIM) if mode == "n128" else (g64_ref, DIFF_DIM)
                    ssq = jnp.dot((x * x).astype(BF16), gm[...], preferred_element_type=F32)
                    gain = gain_ref[:, tile * TN + c:tile * TN + c + MXU_DIM]
                    y = x * lax.rsqrt(ssq * (1.0 / hd) + EPS) * gain
                if mode == "n64r":
                    lane = lax.broadcasted_iota(jnp.int32, y.shape, 1)
                    first = (lane % DIFF_DIM) < (DIFF_DIM // 2)
                    cos = jnp.concatenate([cos_ref[rs, :]] * (MXU_DIM // LANES), axis=1)
                    sin = jnp.concatenate([sin_ref[rs, :]] * (MXU_DIM // LANES), axis=1)
                    partner = jnp.where(first, pltpu.roll(y, MXU_DIM - DIFF_DIM // 2, 1),
                                        pltpu.roll(y, DIFF_DIM // 2, 1))
                    y = y * cos + partner * sin
                ds = slice(d0 + c - c0, d0 + c - c0 + MXU_DIM)
                if transposed:
                    dst_ref[ds, rs] = y.T.astype(dst_ref.dtype)
                else:
                    dst_ref[rs, ds] = y.astype(dst_ref.dtype)

    plan = (
        ((0, WA, "n128", q_ref, 0),),
        ((0, WB, "n128", q_ref, WA), (WB, WB + WC, "n64r", q_ref, WA + WB)),
        ((0, WA, "n128", ka_ref, 0),),
        ((0, WA, "id", va_ref, 0),),
        ((0, WB, "n128", kbvb_ref, 0), (WB, 2 * WB, "id", kbvb_ref, WB)),
        ((0, WC, "n64r", kc_ref, 0, kc_transposed), (WC, 2 * WC, "id", vc_ref, 0)),
    )
    for jj in range(len(plan) + 1):
        @pl.when(j == jj)
        def _(jj=jj):
            if jj < len(plan):
                acc_ref[jj % 2] = jnp.dot(h_ref[...], w_ref[...], preferred_element_type=F32)
            if jj > 0:
                for piece in plan[jj - 1]:
                    finish(jj - 1, *piece)


def _project(x, g_mix, prm, cos_t, sin_t, bufs, *, tm, layer, kc_rows=None):
    n, d = x.shape
    period = cos_t.shape[0] // tm
    n_tiles = P_WIDTH // TN
    depth = bufs[0].shape[0]
    stack = lambda width: jax.ShapeDtypeStruct((depth, n, width), F32)
    lay = lambda width: pl.BlockSpec((None, tm, width), lambda i, j: (layer, i, 0))
    const = lambda shape: pl.BlockSpec(shape, lambda i, j: (0,) * len(shape))
    if kc_rows is None:
        kc_shape, kc_spec = stack(WC), lay(WC)
    else:
        per = kc_rows // tm
        kc_shape = jax.ShapeDtypeStruct((depth, n // kc_rows, WC, kc_rows), F32)
        kc_spec = pl.BlockSpec((None, None, WC, tm), lambda i, j: (layer, i // per, 0, i % per))
    in_specs = [
        pl.BlockSpec((tm, d), lambda i, j: (i, 0)),
        const((1, d)),
        pl.BlockSpec((d, TN), lambda i, j: (0, jnp.minimum(j, n_tiles - 1))),
        const((d, LANES)),
        const((1, LANES)),
        const((1, P_WIDTH)),
        pl.BlockSpec((tm, LANES), lambda i, j: (i % period, 0)),
        pl.BlockSpec((tm, LANES), lambda i, j: (i % period, 0)),
        const((MXU_DIM, MXU_DIM)),
        const((MXU_DIM, MXU_DIM)),
    ] + [pl.BlockSpec(memory_space=pl.ANY)] * N_ALIAS
    assert len(bufs) == N_ALIAS
    out = pl.pallas_call(
        functools.partial(_proj_kernel, tm=tm, kc_transposed=kc_rows is not None),
        grid=(n // tm, n_tiles + 1),
        in_specs=in_specs,
        out_specs=[
            pl.BlockSpec((tm, Q_WIDTH), lambda i, j: (i, 0)),
            pl.BlockSpec((tm, 2 * WB), lambda i, j: (i, 0)),
            lay(WA), lay(WA), kc_spec, lay(WC), lay(H_A),
        ],
        out_shape=[
            jax.ShapeDtypeStruct((n, Q_WIDTH), BF16),
            jax.ShapeDtypeStruct((n, 2 * WB), F32),
            stack(WA), stack(WA), kc_shape, stack(WC), stack(H_A),
        ],
        input_output_aliases={10 + k: 2 + k for k in range(N_ALIAS)},
        scratch_shapes=[pltpu.VMEM((tm, d), BF16), pltpu.VMEM((2, tm, TN), F32)],
        compiler_params=_params("parallel", "arbitrary"),
        name="proj",
    )(x, g_mix, prm["w"], prm["wf"], prm["bf"], prm["gain"], cos_t, sin_t, prm["g128"], prm["g64"], *bufs)
    return out[0], out[1], tuple(out[2:])


def _cumsum_kernel(x_ref, o_ref):
    x = x_ref[...]
    n = x.shape[1]
    lane = lax.broadcasted_iota(jnp.int32, x.shape, 1)
    s = 1
    while s < n:
        x = x + jnp.where(lane >= s, pltpu.roll(x, s, 1), 0.0)
        s *= 2
    o_ref[...] = x


def _cumsum_lanes(x):
    r, n = x.shape
    rb = 8
    return pl.pallas_call(
        _cumsum_kernel,
        grid=(r // rb,),
        in_specs=[pl.BlockSpec((rb, n), lambda i: (i, 0))],
        out_specs=pl.BlockSpec((rb, n), lambda i: (i, 0)),
        out_shape=jax.ShapeDtypeStruct((r, n), F32),
        compiler_params=_params("parallel"),
        name="cumsum",
    )(x)


V_ROWS = HEAD_DIM + 16


def _flash_t(heads, mask_fn, *, tk, n_full):
    for _, _, _, m_ref, acc_ref, _ in heads:
        m_ref[...] = jnp.full(m_ref.shape, -jnp.inf, F32)
        acc_ref[...] = jnp.zeros(acc_ref.shape, F32)

    def scores(kb, slot):
        ks = pl.ds(pl.multiple_of(kb * tk, tk), tk)
        for q, kbf, _, _, _, s_ref in heads:
            s_ref[slot] = lax.dot_general(kbf[ks, :], q, (((1,), (1,)), ((), ())), preferred_element_type=F32)

    def softmax_pv(kb, slot, masked):
        for _, _, vt, m_ref, acc_ref, s_ref in heads:
            st = s_ref[slot]
            if masked:
                st = mask_fn(st)
            m_prev = m_ref[...]
            m_new = jnp.maximum(m_prev, jnp.max(st, axis=0, keepdims=True))
            m_ref[...] = m_new
            pt = jnp.exp2(st - m_new).astype(BF16)
            alpha = jnp.exp2(m_prev - m_new)
            acc_ref[...] = alpha * acc_ref[...] + jnp.dot(vt[kb], pt, preferred_element_type=F32)

    scores(0, 0)

    def pair(jp, carry):
        scores(2 * jp + 1, 1)
        softmax_pv(2 * jp, 0, False)
        scores(2 * jp + 2, 0)
        softmax_pv(2 * jp + 1, 1, False)
        return carry

    lax.fori_loop(0, n_full // 2, pair, 0)
    odd = n_full % 2 == 1

    @pl.when(odd)
    def _():
        scores(n_full, 1)
        softmax_pv(n_full - 1, 0, False)
        softmax_pv(n_full, 1, True)

    @pl.when(jnp.logical_not(odd))
    def _():
        softmax_pv(n_full, 0, True)


def _fill_vt(vt, v_ref, tk):
    ones_row = jnp.where(lax.broadcasted_iota(jnp.int32, (V_ROWS - HEAD_DIM, tk), 0) == 0, 1.0, 0.0)
    for kb in range(vt.shape[0]):
        vt[kb, :HEAD_DIM, :] = v_ref[kb * tk:(kb + 1) * tk, :].T.astype(BF16)
        vt[kb, HEAD_DIM:, :] = ones_row.astype(BF16)


def _split3(c):
    hi = c.astype(BF16).astype(F32)
    r1 = c - hi
    mid = r1.astype(BF16).astype(F32)
    lo = (r1 - mid).astype(BF16).astype(F32)
    return hi, mid, lo


def _decay_lanes(c_rep, key_side):
    hi, mid, lo = _split3(c_rep)
    lane = lax.broadcasted_iota(jnp.int32, c_rep.shape, 1)
    if key_side:
        parts = (-hi, -mid, -lo, 1.0, 1.0, 1.0)
    else:
        parts = (1.0, 1.0, 1.0, hi, mid, lo)
    vals = jnp.zeros(c_rep.shape, F32)
    for idx, part in enumerate(parts):
        vals = jnp.where(lane == idx, part, vals)
    return vals.astype(BF16)


HEAD_GROUP = 2


def _fox_kernel(q_ref, k_ref, v_ref, crow_ref, o_ref, kbf, vt, crep, m_ref, acc_ref, s_ref, *, tq):
    n_blocks = vt.shape[1]
    for g in range(HEAD_GROUP):
        gs = slice(g * HEAD_DIM, (g + 1) * HEAD_DIM)
        kbf[g, :, :HEAD_DIM] = k_ref[:, gs].astype(BF16)
        _fill_vt(vt.at[g], v_ref.at[:, gs], tq)
        for kb in range(n_blocks):
            rs = slice(kb * tq, (kb + 1) * tq)
            c_rep = jnp.broadcast_to(crow_ref[g, :, rs] * LOG2E, (LANES, tq)).T
            crep[g, rs, :] = c_rep
            kbf[g, rs, HEAD_DIM:] = _decay_lanes(c_rep, True)

    def mask_fn(st):
        kpos = lax.broadcasted_iota(jnp.int32, st.shape, 0)
        qpos = lax.broadcasted_iota(jnp.int32, st.shape, 1)
        return jnp.where(kpos <= qpos, st, NEG)

    def q_block(i, carry):
        rows = pl.ds(pl.multiple_of(i * tq, tq), tq)
        heads = []
        for g in range(HEAD_GROUP):
            gs = slice(g * HEAD_DIM, (g + 1) * HEAD_DIM)
            q = jnp.concatenate([q_ref[rows, gs], _decay_lanes(crep[g, rows, :], False)], axis=1)
            heads.append((q, kbf.at[g], vt.at[g], m_ref.at[g], acc_ref.at[g], s_ref.at[g]))
        _flash_t(heads, mask_fn, tk=tq, n_full=i)
        for g in range(HEAD_GROUP):
            acc = acc_ref[g]
            o_ref[rows, g * HEAD_DIM:(g + 1) * HEAD_DIM] = \
                (acc[:HEAD_DIM] / acc[HEAD_DIM:HEAD_DIM + 1]).T.astype(BF16)
        return carry

    lax.fori_loop(0, n_blocks, q_block, 0)


def _fox_prompt(q3, ka, va, c, *, layer, tq):
    b, t, _ = q3.shape
    crow = c.reshape(b, H_A, 1, t)
    gw = HEAD_GROUP * HEAD_DIM
    kv = lambda: pl.BlockSpec((None, None, t, gw), lambda bi, h: (layer, bi, 0, h))
    return pl.pallas_call(
        functools.partial(_fox_kernel, tq=tq),
        grid=(b, H_A // HEAD_GROUP),
        in_specs=[
            pl.BlockSpec((None, t, gw), lambda bi, h: (bi, 0, h)),
            kv(), kv(),
            pl.BlockSpec((None, HEAD_GROUP, 1, t), lambda bi, h: (bi, h, 0, 0)),
        ],
        out_specs=pl.BlockSpec((None, t, gw), lambda bi, h: (bi, 0, h)),
        out_shape=jax.ShapeDtypeStruct((b, t, WA), BF16),
        scratch_shapes=[
            pltpu.VMEM((HEAD_GROUP, t, 2 * HEAD_DIM), BF16),
            pltpu.VMEM((HEAD_GROUP, t // tq, V_ROWS, tq), BF16),
            pltpu.VMEM((HEAD_GROUP, t, LANES), F32),
            pltpu.VMEM((HEAD_GROUP, 1, tq), F32), pltpu.VMEM((HEAD_GROUP, V_ROWS, tq), F32),
            pltpu.VMEM((HEAD_GROUP, 2, tq, tq), F32),
        ],
        compiler_params=_params("parallel", "parallel"),
        name="fox_prompt",
    )(q3, ka, va, crow)


def _lambda(lq1_ref, lk1_ref, lq2_ref, lk2_ref, lam_init):
    a = jnp.sum(lq1_ref[...] * lk1_ref[...], axis=-1, keepdims=True)
    b = jnp.sum(lq2_ref[...] * lk2_ref[...], axis=-1, keepdims=True)
    return jnp.exp(a) - jnp.exp(b) + lam_init


def _stack_maps(q):
    lane = lax.broadcasted_iota(jnp.int32, q.shape, 1)
    lo = lane < DIFF_DIM
    zero = jnp.zeros_like(q)
    return jnp.concatenate([jnp.where(lo, q, zero), jnp.where(lo, zero, q)], axis=0)


def _subln(o, g, lam_init):
    return (_rms_rows(o, g) * (1.0 - lam_init)).astype(BF16)


def _diff_kernel(q_ref, kt_ref, v_ref, lq1_ref, lk1_ref, lq2_ref, lk2_ref, g_ref, o_ref,
                 kbf, vt, m_ref, acc_ref, s_ref, *, tq, lam_init):
    n_blocks = vt.shape[1]
    for g in range(HEAD_GROUP):
        gs = slice(g * HEAD_DIM, (g + 1) * HEAD_DIM)
        for kb in range(n_blocks):
            rs = slice(kb * tq, (kb + 1) * tq)
            kbf[g, rs, :] = kt_ref[gs, rs].T.astype(BF16)
        _fill_vt(vt.at[g], v_ref.at[:, gs], tq)
    lam = _lambda(lq1_ref, lk1_ref, lq2_ref, lk2_ref, lam_init)

    def mask_fn(st):
        kpos = lax.broadcasted_iota(jnp.int32, st.shape, 0)
        qpos = lax.broadcasted_iota(jnp.int32, st.shape, 1) % tq
        return jnp.where(kpos // CHUNK <= qpos // CHUNK, st, NEG)

    def q_block(i, carry):
        rows = pl.ds(pl.multiple_of(i * tq, tq), tq)
        heads = [(_stack_maps(q_ref[rows, g * HEAD_DIM:(g + 1) * HEAD_DIM]),
                  kbf.at[g], vt.at[g], m_ref.at[g], acc_ref.at[g], s_ref.at[g]) for g in range(HEAD_GROUP)]
        _flash_t(heads, mask_fn, tk=tq, n_full=i)
        for g in range(HEAD_GROUP):
            acc = acc_ref[g]
            o = acc[:HEAD_DIM] / acc[HEAD_DIM:HEAD_DIM + 1]
            o_ref[rows, g * HEAD_DIM:(g + 1) * HEAD_DIM] = \
                _subln((o[:, :tq] - lam * o[:, tq:]).T, g_ref[...], lam_init)
        return carry

    lax.fori_loop(0, n_blocks, q_block, 0)


def _lam_specs():
    return [pl.BlockSpec((1, DIFF_DIM), lambda *a: (0, 0)) for _ in range(4)] + \
           [pl.BlockSpec((1, LANES), lambda *a: (0, 0))]


def _diff_prompt(q3, kct, vc, lams, g_subln, *, layer, tq, lam_init):
    b, t, _ = q3.shape
    gw = HEAD_GROUP * HEAD_DIM
    return pl.pallas_call(
        functools.partial(_diff_kernel, tq=tq, lam_init=lam_init),
        grid=(b, H_C // HEAD_GROUP),
        in_specs=[
            pl.BlockSpec((None, t, gw), lambda bi, h: (bi, 0, QC_BLK // HEAD_GROUP + h)),
            pl.BlockSpec((None, None, gw, t), lambda bi, h: (layer, bi, h, 0)),
            pl.BlockSpec((None, None, t, gw), lambda bi, h: (layer, bi, 0, h)),
        ] + _lam_specs(),
        out_specs=pl.BlockSpec((None, t, gw), lambda bi, h: (bi, 0, h)),
        out_shape=jax.ShapeDtypeStruct((b, t, WC), BF16),
        scratch_shapes=[
            pltpu.VMEM((HEAD_GROUP, t, HEAD_DIM), BF16), pltpu.VMEM((HEAD_GROUP, t // tq, V_ROWS, tq), BF16),
            pltpu.VMEM((HEAD_GROUP, 1, 2 * tq), F32), pltpu.VMEM((HEAD_GROUP, V_ROWS, 2 * tq), F32),
            pltpu.VMEM((HEAD_GROUP, 2, tq, 2 * tq), F32),
        ],
        compiler_params=_params("parallel", "parallel"),
        name="diff_prompt",
    )(q3, kct, vc, *lams, g_subln)


BIAS_EXT = 5 * LANES


def _band_bias_tile(ext):
    x = jnp.broadcast_to(ext, (CHUNK, BIAS_EXT))
    return pltpu.roll(x, BIAS_EXT - (CHUNK - 1), 1, stride=1, stride_axis=0)[:, :BAND_KEYS]


def _band_chunk(q, k, v, bias, first_key_pos):
    s = lax.dot_general(q, k, (((1,), (1,)), ((), ())), preferred_element_type=F32) + bias
    kpos = first_key_pos + lax.broadcasted_iota(jnp.int32, s.shape, 1)
    s = jnp.where(kpos >= 0, s, NEG)
    m = jnp.max(s, axis=1, keepdims=True)
    p = jnp.exp2(s - m)
    l = jnp.sum(p, axis=1, keepdims=True)
    return (jnp.dot(p.astype(BF16), v, preferred_element_type=F32) / l).astype(BF16)


BAND_STEP = WINDOW_B
BM_SPAN = 5 * LANES
BM_EXT = BM_SPAN + (2 * BAND_STEP // LANES - 1) * LANES


def _band_biasmask(g_ref, bm_ref):
    n_kb = 2 * BAND_STEP // LANES
    for kb in range(n_kb):
        start = LANES * (n_kb - 1 - kb)
        x = jnp.broadcast_to(g_ref[:, start:start + BM_SPAN], (LANES, BM_SPAN))
        tile = pltpu.roll(x, BM_SPAN - (LANES - 1), 1, stride=1, stride_axis=0)[:, :BAND_STEP]
        k = kb * LANES + lax.broadcasted_iota(jnp.int32, (LANES, BAND_STEP), 0)
        first = (lax.broadcasted_iota(jnp.int32, (LANES, BAND_STEP), 1) // CHUNK) * CHUNK
        tile = jnp.where(k >= first, tile, NEG)
        bm_ref[kb * LANES:(kb + 1) * LANES, :] = jnp.where(k < first + BAND_KEYS, tile, NEG)


N_BAND_ALIAS = 2


def _band_kernel(*refs, t):
    q_ref, k_ref, v_ref, g_ref = refs[:4]
    o_ref, pbk_ref, pbv_ref, kpad, vt, bm_ref = refs[4 + N_BAND_ALIAS:]
    h = pl.program_id(1)
    i = pl.program_id(2)

    @pl.when(i == 0)
    def _():
        kpad[:BAND_STEP, :] = jnp.zeros((BAND_STEP, LANES), BF16)
        kpad[BAND_STEP:, :] = k_ref[...].astype(BF16)
        ones_row = jnp.where(lax.broadcasted_iota(jnp.int32, (V_ROWS - HEAD_DIM, BAND_STEP), 0) == 0, 1.0, 0.0)
        vt[0] = jnp.zeros((V_ROWS, BAND_STEP), BF16)
        for kb in range(t // BAND_STEP):
            vt[kb + 1, :HEAD_DIM, :] = v_ref[kb * BAND_STEP:(kb + 1) * BAND_STEP, :].T.astype(BF16)
            vt[kb + 1, HEAD_DIM:, :] = ones_row.astype(BF16)
        _band_biasmask(g_ref, bm_ref)
        for hh in range(H_B):
            @pl.when(h == hh)
            def _(hh=hh):
                pbk_ref[pl.ds(hh, WINDOW_B, stride=H_B), :] = k_ref[t - WINDOW_B:, :]
                pbv_ref[pl.ds(hh, WINDOW_B, stride=H_B), :] = v_ref[t - WINDOW_B:, :]

    ks = pl.ds(pl.multiple_of(i * BAND_STEP, BAND_STEP), 2 * BAND_STEP)
    st = lax.dot_general(kpad[ks, :], q_ref[...], (((1,), (1,)), ((), ())),
                         preferred_element_type=F32) + bm_ref[...]
    kpos = (i - 1) * BAND_STEP + lax.broadcasted_iota(jnp.int32, st.shape, 0)
    st = jnp.where(kpos >= 0, st, NEG)
    m = jnp.max(st, axis=0, keepdims=True)
    p = jnp.exp2(st - m).astype(BF16)
    acc = jnp.dot(vt[i], p[:BAND_STEP], preferred_element_type=F32) + \
        jnp.dot(vt[i + 1], p[BAND_STEP:], preferred_element_type=F32)
    o_ref[...] = (acc[:HEAD_DIM] / acc[HEAD_DIM:HEAD_DIM + 1]).T.astype(BF16)


def _band_prompt(q3, kbvb, g, bufs, *, layer):
    b, t, _ = q3.shape
    assert len(bufs) == N_BAND_ALIAS
    keep = jax.ShapeDtypeStruct(bufs[0].shape, F32)
    keep_spec = pl.BlockSpec((None, None, WINDOW_B * H_B, HEAD_DIM), lambda bi, h, i: (layer, bi, 0, 0))
    out = pl.pallas_call(
        functools.partial(_band_kernel, t=t),
        grid=(b, H_B, t // BAND_STEP),
        in_specs=[
            pl.BlockSpec((None, BAND_STEP, LANES), lambda bi, h, i: (bi, i, QB_BLK + h)),
            pl.BlockSpec((None, t, LANES), lambda bi, h, i: (bi, 0, h)),
            pl.BlockSpec((None, t, LANES), lambda bi, h, i: (bi, 0, H_B + h)),
            pl.BlockSpec((None, 1, BM_EXT), lambda bi, h, i: (h, 0, 0)),
        ] + [pl.BlockSpec(memory_space=pl.ANY)] * N_BAND_ALIAS,
        out_specs=[pl.BlockSpec((None, BAND_STEP, LANES), lambda bi, h, i: (bi, i, h)), keep_spec, keep_spec],
        out_shape=[jax.ShapeDtypeStruct((b, t, WB), BF16), keep, keep],
        input_output_aliases={4 + k: 1 + k for k in range(N_BAND_ALIAS)},
        scratch_shapes=[pltpu.VMEM((t + BAND_STEP, LANES), BF16),
                        pltpu.VMEM((t // BAND_STEP + 1, V_ROWS, BAND_STEP), BF16),
                        pltpu.VMEM((2 * BAND_STEP, BAND_STEP), F32)],
        compiler_params=_params("parallel", "arbitrary", "arbitrary"),
        name="band_prompt",
    )(q3, kbvb, kbvb, g, *bufs)
    return out[0], tuple(out[1:])


SAMPLE_KV = 1024


def _qk(q, k):
    return lax.dot_general(q, k, (((1,), (1,)), ((), ())), preferred_element_type=F32)


def _online_rows(s, v, m_ref, l_ref, acc_ref, h):
    m_prev = m_ref[h]
    m_new = jnp.maximum(m_prev, jnp.max(s, axis=1, keepdims=True))
    alpha = jnp.exp2(m_prev - m_new)
    p = jnp.exp2(s - m_new)
    l_ref[h] = alpha * l_ref[h] + jnp.sum(p, axis=1, keepdims=True)
    acc_ref[h] = alpha * acc_ref[h] + jnp.dot(p.astype(BF16), v, preferred_element_type=F32)
    m_ref[h] = m_new


def _reset_rows(m_ref, l_ref, acc_ref):
    m_ref[...] = jnp.full(m_ref.shape, -jnp.inf, F32)
    l_ref[...] = jnp.zeros(l_ref.shape, F32)
    acc_ref[...] = jnp.zeros(acc_ref.shape, F32)


def _fox_sample_kernel(q_ref, kn_ref, vn_ref, kc_ref, vc_ref, crow_ref, cnew_ref, ccol_ref, o_ref,
                       m_ref, l_ref, acc_ref):
    j = pl.program_id(1)
    n = q_ref.shape[0]

    @pl.when(j == 0)
    def _():
        _reset_rows(m_ref, l_ref, acc_ref)

    for h in range(H_A):
        hs = slice(h * HEAD_DIM, (h + 1) * HEAD_DIM)
        rows = pl.ds(h, SAMPLE_KV, stride=H_A)
        s = _qk(q_ref[:, hs], kc_ref[rows, :].astype(BF16)) + \
            (ccol_ref[h] - crow_ref[h:h + 1, :]) * LOG2E
        _online_rows(s, vc_ref[rows, :].astype(BF16), m_ref, l_ref, acc_ref, h)

    @pl.when(j == pl.num_programs(1) - 1)
    def _():
        row = lax.broadcasted_iota(jnp.int32, (n, n), 0)
        col = lax.broadcasted_iota(jnp.int32, (n, n), 1)
        for h in range(H_A):
            hs = slice(h * HEAD_DIM, (h + 1) * HEAD_DIM)
            s = _qk(q_ref[:, hs], kn_ref[:, hs].astype(BF16)) + \
                (ccol_ref[h] - cnew_ref[h:h + 1, :n]) * LOG2E
            _online_rows(jnp.where(col <= row, s, NEG), vn_ref[:, hs].astype(BF16), m_ref, l_ref, acc_ref, h)
            o_ref[:, hs] = (acc_ref[h] / l_ref[h]).astype(BF16)


def _sample_scratch(heads, rows):
    return [pltpu.VMEM((heads, rows, 1), F32), pltpu.VMEM((heads, rows, 1), F32),
            pltpu.VMEM((heads, rows, HEAD_DIM), F32)]


def _fox_sample(q3, kn, vn, ck, cv, c, *, layer):
    b, n, _ = q3.shape
    past = ck.shape[2] // H_A
    new = lambda: pl.BlockSpec((None, None, n, WA), lambda bi, j: (layer, bi, 0, 0))
    cache = lambda: pl.BlockSpec((None, None, SAMPLE_KV * H_A, HEAD_DIM), lambda bi, j: (layer, bi, j, 0))
    ccol = c[:, :, past:past + n].reshape(b, H_A, n, 1)
    return pl.pallas_call(
        _fox_sample_kernel,
        grid=(b, past // SAMPLE_KV),
        in_specs=[
            pl.BlockSpec((None, n, WA), lambda bi, j: (bi, 0, 0)),
            new(), new(), cache(), cache(),
            pl.BlockSpec((None, H_A, SAMPLE_KV), lambda bi, j: (bi, 0, j)),
            pl.BlockSpec((None, H_A, LANES), lambda bi, j: (bi, 0, past // LANES)),
            pl.BlockSpec((None, H_A, n, 1), lambda bi, j: (bi, 0, 0, 0)),
        ],
        out_specs=pl.BlockSpec((None, n, WA), lambda bi, j: (bi, 0, 0)),
        out_shape=jax.ShapeDtypeStruct((b, n, WA), BF16),
        scratch_shapes=_sample_scratch(H_A, n),
        compiler_params=_params("parallel", "arbitrary"),
        name="fox_sample",
    )(q3, kn, vn, ck, cv, c, c, ccol)


def _band_sample_kernel(*refs, first_key_pos):
    q_ref, kbvb_ref, kc_ref, vc_ref, ext_ref = refs[:5]
    o_ref, sbk_ref, sbv_ref = refs[5 + N_BAND_ALIAS:]
    n = q_ref.shape[0]
    rows = kc_ref.shape[0] // H_B
    keep = rows - n
    sbk_ref[:keep * H_B, :] = kc_ref[n * H_B:, :]
    sbv_ref[:keep * H_B, :] = vc_ref[n * H_B:, :]
    for h in range(H_B):
        hs = slice(h * HEAD_DIM, (h + 1) * HEAD_DIM)
        kn = kbvb_ref[:, hs]
        vn = kbvb_ref[:, WB + h * HEAD_DIM:WB + (h + 1) * HEAD_DIM]
        new_rows = pl.ds(keep * H_B + h, n, stride=H_B)
        sbk_ref[new_rows, :] = kn
        sbv_ref[new_rows, :] = vn
        cached = pl.ds(h, rows, stride=H_B)
        k = jnp.concatenate([kc_ref[cached, :], kn], axis=0).astype(BF16)
        v = jnp.concatenate([vc_ref[cached, :], vn], axis=0).astype(BF16)
        o_ref[:, hs] = _band_chunk(q_ref[:, hs], k, v, _band_bias_tile(ext_ref[h]), first_key_pos)


def _band_sample(q3, kbvb, ck, cv, ext, bufs, *, layer, first_key_pos):
    b, n, _ = q3.shape
    rows_h = ck.shape[2]
    assert len(bufs) == N_BAND_ALIAS
    roll_spec = lambda: pl.BlockSpec((None, None, rows_h, HEAD_DIM), lambda bi: (layer, bi, 0, 0))
    rolled = jax.ShapeDtypeStruct(ck.shape, F32)
    out = pl.pallas_call(
        functools.partial(_band_sample_kernel, first_key_pos=first_key_pos),
        grid=(b,),
        in_specs=[
            pl.BlockSpec((None, n, WB), lambda bi: (bi, 0, WA // WB)),
            pl.BlockSpec((None, n, 2 * WB), lambda bi: (bi, 0, 0)),
            roll_spec(), roll_spec(),
            pl.BlockSpec((H_B, 1, BIAS_EXT), lambda bi: (0, 0, 0)),
        ] + [pl.BlockSpec(memory_space=pl.ANY)] * N_BAND_ALIAS,
        out_specs=[pl.BlockSpec((None, n, WB), lambda bi: (bi, 0, 0)), roll_spec(), roll_spec()],
        out_shape=[jax.ShapeDtypeStruct((b, n, WB), BF16), rolled, rolled],
        input_output_aliases={5 + k: 1 + k for k in range(N_BAND_ALIAS)},
        compiler_params=_params("parallel"),
        name="band_sample",
    )(q3, kbvb, ck, cv, ext, *bufs)
    return out[0], tuple(out[1:])


def _diff_sample_kernel(q_ref, kn_ref, vn_ref, kt_ref, vc_ref, lq1_ref, lk1_ref, lq2_ref, lk2_ref,
                        g_ref, o_ref, m_ref, l_ref, acc_ref, *, lam_init):
    j = pl.program_id(1)
    n = q_ref.shape[0]

    @pl.when(j == 0)
    def _():
        _reset_rows(m_ref, l_ref, acc_ref)

    for h in range(H_C):
        q = q_ref[:, h * HEAD_DIM:(h + 1) * HEAD_DIM]
        s = jnp.concatenate(
            [jnp.dot(q[:, :DIFF_DIM], kt_ref[h, 0].astype(BF16), preferred_element_type=F32),
             jnp.dot(q[:, DIFF_DIM:], kt_ref[h, 1].astype(BF16), preferred_element_type=F32)], axis=0)
        _online_rows(s, vc_ref[pl.ds(h, SAMPLE_KV, stride=H_C), :].astype(BF16), m_ref, l_ref, acc_ref, h)

    @pl.when(j == pl.num_programs(1) - 1)
    def _():
        lam = _lambda(lq1_ref, lk1_ref, lq2_ref, lk2_ref, lam_init)
        for h in range(H_C):
            hs = slice(h * HEAD_DIM, (h + 1) * HEAD_DIM)
            s = _qk(_stack_maps(q_ref[:, hs]), kn_ref[:, hs].astype(BF16))
            _online_rows(s, vn_ref[:, hs].astype(BF16), m_ref, l_ref, acc_ref, h)
            o = acc_ref[h] / l_ref[h]
            o_ref[:, hs] = _subln(o[:n] - lam * o[n:], g_ref[...], lam_init)


def _diff_sample(q3, kn, vn, ckt, cv, lams, g_subln, *, layer, lam_init):
    b, n, _ = q3.shape
    past = cv.shape[2] // H_C
    new = lambda: pl.BlockSpec((None, None, n, WC), lambda bi, j: (layer, bi, 0, 0))
    return pl.pallas_call(
        functools.partial(_diff_sample_kernel, lam_init=lam_init),
        grid=(b, past // SAMPLE_KV),
        in_specs=[
            pl.BlockSpec((None, n, WC), lambda bi, j: (bi, 0, (WA + WB) // WC)),
            new(), new(),
            pl.BlockSpec((None, None, H_C, 2, DIFF_DIM, SAMPLE_KV), lambda bi, j: (layer, bi, 0, 0, 0, j)),
            pl.BlockSpec((None, None, SAMPLE_KV * H_C, HEAD_DIM), lambda bi, j: (layer, bi, j, 0)),
        ] + _lam_specs(),
        out_specs=pl.BlockSpec((None, n, WC), lambda bi, j: (bi, 0, 0)),
        out_shape=jax.ShapeDtypeStruct((b, n, WC), BF16),
        scratch_shapes=_sample_scratch(H_C, 2 * n),
        compiler_params=_params("parallel", "arbitrary"),
        name="diff_sample",
    )(q3, kn, vn, ckt, cv, *lams, g_subln)


def _merge_kernel(x_ref, oa_ref, ob_ref, oc_ref, w_ref, o_ref):
    o = jnp.concatenate([oa_ref[...], ob_ref[...], oc_ref[...]], axis=1)
    o_ref[...] = x_ref[...] + jnp.dot(o, w_ref[...], preferred_element_type=F32)


def _merge(x, oa, ob, oc, w, *, tm):
    n, d = x.shape
    mix = w.shape[0]
    return pl.pallas_call(
        _merge_kernel,
        grid=(n // tm,),
        in_specs=[
            pl.BlockSpec((tm, d), lambda i: (i, 0)),
            pl.BlockSpec((tm, oa.shape[1]), lambda i: (i, 0)),
            pl.BlockSpec((tm, ob.shape[1]), lambda i: (i, 0)),
            pl.BlockSpec((tm, oc.shape[1]), lambda i: (i, 0)),
            pl.BlockSpec((mix, d), lambda i: (0, 0)),
        ],
        out_specs=pl.BlockSpec((tm, d), lambda i: (i, 0)),
        out_shape=jax.ShapeDtypeStruct((n, d), F32),
        compiler_params=_params("parallel"),
        name="merge",
    )(x, oa, ob, oc, w)


def _mlp_kernel(x_ref, g_ref, wu_ref, wd_ref, o_ref, h_ref):
    @pl.when(pl.program_id(1) == 0)
    def _():
        x = x_ref[...]
        h_ref[...] = _rms_rows(x, g_ref[...]).astype(BF16)
        o_ref[...] = x

    u = jnp.dot(h_ref[...], wu_ref[...], preferred_element_type=F32)
    a = jnp.square(jnp.maximum(u, 0.0)).astype(BF16)
    o_ref[...] += jnp.dot(a, wd_ref[...], preferred_element_type=F32)


def _mlp(x, g, wu, wd, *, tm, tf):
    n, d = x.shape
    ff = wu.shape[1]
    return pl.pallas_call(
        _mlp_kernel,
        grid=(n // tm, ff // tf),
        in_specs=[
            pl.BlockSpec((tm, d), lambda i, f: (i, 0)),
            pl.BlockSpec((1, d), lambda i, f: (0, 0)),
            pl.BlockSpec((d, tf), lambda i, f: (0, f)),
            pl.BlockSpec((tf, d), lambda i, f: (f, 0)),
        ],
        out_specs=pl.BlockSpec((tm, d), lambda i, f: (i, 0)),
        out_shape=jax.ShapeDtypeStruct((n, d), F32),
        scratch_shapes=[pltpu.VMEM((tm, d), BF16)],
        compiler_params=_params("parallel", "arbitrary"),
        name="mlp",
    )(x, g, wu, wd)


def _split_cols(w):
    sizes = [WA] * 3 + [H_A] + [WB] * 3 + [WC] * 3
    out, c = [], 0
    for s in sizes:
        out.append(w[..., c:c + s])
        c += s
    return out


def _rope_tables(pos):
    half = DIFF_DIM // 2
    inv = ROPE_THETA ** (-jnp.arange(half, dtype=F32) * 2.0 / DIFF_DIM)
    ang = pos.astype(F32)[:, None] * inv[None, :]
    cos, sin = jnp.cos(ang), jnp.sin(ang)
    cos_t = jnp.tile(cos, (1, LANES // half))
    sin_t = jnp.tile(jnp.concatenate([-sin, sin], axis=1), (1, LANES // DIFF_DIM))
    return cos_t, sin_t


def _band_bias_ext(rel_table):
    far = WINDOW_B - REL_MAX_PAST + (CHUNK - 1)
    tab = rel_table.astype(F32) * LOG2E
    ext = jnp.concatenate(
        [jnp.broadcast_to(tab[:, -1:], (tab.shape[0], far)), tab[:, ::-1],
         jnp.broadcast_to(tab[:, :1], (tab.shape[0], BIAS_EXT - far - tab.shape[1]))], axis=1)
    return ext.reshape(tab.shape[0], 1, BIAS_EXT)


def _band_bias_reversed(ext):
    n_off = BAND_KEYS + CHUNK - 1
    top = CHUNK - 1 + LANES - 1 + BM_EXT - BM_SPAN
    rev = ext[:, :, :n_off][:, :, ::-1]
    lead = top - (n_off - 1)
    return jnp.pad(rev, ((0, 0), (0, 0), (lead, BM_EXT - lead - n_off)))


def _block_diag_ones(block):
    r = jnp.arange(MXU_DIM)
    return (r[:, None] // block == r[None, :] // block).astype(BF16)


def _pick(n, pref):
    return pref if n % pref == 0 else n


def kernel(x_prompt, x_sample, cache_a_k, cache_a_v, cache_a_logf, cache_b_k, cache_b_v, cache_c_k, cache_c_v, w_in, b_f, g_qa, g_ka, g_qb, g_kb, rel_bias, g_qc, g_kc, lam_q1, lam_k1, lam_q2, lam_k2, g_subln, w_out, g_mix, g_mlp, w_up, w_down):
    depth = w_in.shape[0]
    bp, t, d = x_prompt.shape
    bs, ns, _ = x_sample.shape
    past = cache_a_k.shape[2]
    b_rows = cache_b_k.shape[2]
    assert ns == CHUNK and b_rows == WINDOW_B and past % CHUNK == 0 and t % WINDOW_B == 0
    assert rel_bias.shape[-1] == REL_MAX_PAST + CHUNK
    n_p, n_s = bp * t, bs * ns

    tm_p = _pick(n_p, 512)
    tm_s = _pick(n_s, 512)
    tq = _pick(t, 512)

    cos_p, sin_p = _rope_tables(jnp.arange(t))
    cos_s, sin_s = _rope_tables(past + jnp.arange(ns))
    cos_s = jnp.tile(cos_s, (tm_s // ns, 1))
    sin_s = jnp.tile(sin_s, (tm_s // ns, 1))
    c_width = -(-(past + ns) // LANES) * LANES
    g128 = _block_diag_ones(HEAD_DIM)
    g64 = _block_diag_ones(DIFF_DIM)

    xp = x_prompt.reshape(n_p, d)
    xs = x_sample.reshape(n_s, d)
    zeros = lambda *shape: jnp.zeros(shape, F32)
    bufs_p = (zeros(depth, n_p, WA), zeros(depth, n_p, WA), zeros(depth, bp, WC, t), zeros(depth, n_p, WC),
              zeros(depth, n_p, H_A))
    bufs_s = (zeros(depth, n_s, WA), zeros(depth, n_s, WA), zeros(depth, n_s, WC), zeros(depth, n_s, WC),
              zeros(depth, n_s, H_A))
    band_p = (zeros(depth, bp, WINDOW_B * H_B, HEAD_DIM), zeros(depth, bp, WINDOW_B * H_B, HEAD_DIM))
    band_s = (zeros(depth, bs, b_rows * H_B, HEAD_DIM), zeros(depth, bs, b_rows * H_B, HEAD_DIM))
    ones = jnp.ones((HEAD_DIM,), F32)
    cache_c_kt = jnp.transpose(cache_c_k, (0, 1, 3, 4, 5, 2))
    cache_logf_rows = jnp.transpose(cache_a_logf.astype(F32), (0, 1, 3, 2))
    frames_by_head = lambda a: a.reshape(a.shape[0], a.shape[1], a.shape[2] * a.shape[3], HEAD_DIM)
    cak, cav, cbk, cbv, ccv = map(frames_by_head, (cache_a_k, cache_a_v, cache_b_k, cache_b_v, cache_c_v))

    for l in range(depth):
        lam_init = 0.8 - 0.6 * math.exp(-0.3 * l)
        qa, ka, va, fa, qb, kb, vb, qc, kc, vc = _split_cols(w_in[l])
        gqc = jnp.tile(g_qc[l], 2)
        gkc = jnp.tile(g_kc[l], 2)
        q_scale = HEAD_DIM ** -0.5 * LOG2E
        prm = dict(
            w=jnp.concatenate([qa, qb, qc, ka, va, kb, vb, kc, vc], axis=1).astype(BF16),
            wf=jnp.pad(fa, ((0, 0), (0, LANES - H_A))).astype(BF16),
            bf=jnp.pad(b_f[l], (0, LANES - H_A)).reshape(1, LANES),
            gain=jnp.concatenate(
                [jnp.tile(g_qa[l] * q_scale, H_A), jnp.tile(g_qb[l] * q_scale, H_B),
                 jnp.tile(gqc * (DIFF_DIM ** -0.5 * LOG2E), H_C),
                 jnp.tile(g_ka[l], H_A), jnp.tile(ones, H_A),
                 jnp.tile(g_kb[l], H_B), jnp.tile(ones, H_B),
                 jnp.tile(gkc, H_C), jnp.tile(ones, H_C)]).reshape(1, P_WIDTH),
            g128=g128, g64=g64)
        gmix = g_mix[l].reshape(1, d)
        gmlp = g_mlp[l].reshape(1, d)
        gsub = g_subln[l].reshape(1, HEAD_DIM)
        lams = [a[l].reshape(1, DIFF_DIM) for a in (lam_q1, lam_k1, lam_q2, lam_k2)]
        wo = w_out[l].astype(BF16)
        wu = w_up[l].astype(BF16)
        wd = w_down[l].astype(BF16)
        ext = _band_bias_ext(rel_bias[l])
        ext_rev = _band_bias_reversed(ext)

        qs, kbvb, bufs_p = _project(xp, gmix, prm, cos_p, sin_p, bufs_p, tm=tm_p, layer=l, kc_rows=t)
        ka_all, va_all, kct_all, vc_all, lf_all = bufs_p
        q3 = qs.reshape(bp, t, Q_WIDTH)
        logf_rows = jnp.transpose(lf_all[l].reshape(bp, t, H_A), (0, 2, 1)).reshape(bp * H_A, t)
        c_p = _cumsum_lanes(logf_rows)
        oa = _fox_prompt(q3, ka_all.reshape(depth, bp, t, WA), va_all.reshape(depth, bp, t, WA), c_p,
                         layer=l, tq=tq)
        ob, band_p = _band_prompt(q3, kbvb.reshape(bp, t, 2 * WB), ext_rev, band_p, layer=l)
        oc = _diff_prompt(q3, kct_all, vc_all.reshape(depth, bp, t, WC),
                          lams, gsub, layer=l, tq=tq, lam_init=lam_init)
        xp = _merge(xp, oa.reshape(n_p, -1), ob.reshape(n_p, -1), oc.reshape(n_p, -1), wo, tm=_pick(n_p, 512))
        xp = _mlp(xp, gmlp, wu, wd, tm=_pick(n_p, 512), tf=1024)

        qs, kbvb, bufs_s = _project(xs, gmix, prm, cos_s, sin_s, bufs_s, tm=tm_s, layer=l)
        ka_all, va_all, kc_all, vc_all, lf_all = bufs_s
        q3 = qs.reshape(bs, ns, Q_WIDTH)
        la_rows = jnp.transpose(lf_all[l].reshape(bs, ns, H_A), (0, 2, 1))
        logf_all = jnp.concatenate(
            [cache_logf_rows[l], la_rows, jnp.zeros((bs, H_A, c_width - past - ns), F32)], axis=2)
        c_s = _cumsum_lanes(logf_all.reshape(bs * H_A, c_width)).reshape(bs, H_A, c_width)
        oa = _fox_sample(q3, ka_all.reshape(depth, bs, ns, WA), va_all.reshape(depth, bs, ns, WA),
                         cak, cav, c_s, layer=l)
        ob, band_s = _band_sample(q3, kbvb.reshape(bs, ns, 2 * WB), cbk, cbv, ext, band_s,
                                  layer=l, first_key_pos=past - b_rows)
        oc = _diff_sample(q3, kc_all.reshape(depth, bs, ns, WC), vc_all.reshape(depth, bs, ns, WC),
                          cache_c_kt, ccv, lams, gsub, layer=l, lam_init=lam_init)
        xs = _merge(xs, oa.reshape(n_s, -1), ob.reshape(n_s, -1), oc.reshape(n_s, -1), wo, tm=_pick(n_s, 512))
        xs = _mlp(xs, gmlp, wu, wd, tm=_pick(n_s, 512), tf=1024)

    pak, pav, pckt, pcv, pal = bufs_p
    sak, sav, sck, scv, sal = bufs_s
    pck = jnp.transpose(pckt.reshape(depth, bp, H_C, 2, DIFF_DIM, t), (0, 1, 5, 2, 3, 4))
    return (xp.reshape(bp, t, d), xs.reshape(bs, ns, d),
            pak.reshape(depth, bp, t, H_A, HEAD_DIM), pav.reshape(depth, bp, t, H_A, HEAD_DIM),
            pal.reshape(depth, bp, t, H_A),
            band_p[0].reshape(depth, bp, WINDOW_B, H_B, HEAD_DIM), band_p[1].reshape(depth, bp, WINDOW_B, H_B, HEAD_DIM),
            pck, pcv.reshape(depth, bp, t, H_C, HEAD_DIM),
            sak.reshape(depth, bs, ns, H_A, HEAD_DIM), sav.reshape(depth, bs, ns, H_A, HEAD_DIM),
            sal.reshape(depth, bs, ns, H_A),
            band_s[0].reshape(depth, bs, b_rows, H_B, HEAD_DIM), band_s[1].reshape(depth, bs, b_rows, H_B, HEAD_DIM),
            sck.reshape(depth, bs, ns, H_C, 2, DIFF_DIM), scv.reshape(depth, bs, ns, H_C, HEAD_DIM))
```

```python
import functools
import math

import jax
import jax.numpy as jnp
from jax import lax
from jax.experimental import pallas as pl
from jax.experimental.pallas import tpu as pltpu

F32 = jnp.float32
BF16 = jnp.bfloat16

CHUNK = 64
HEAD_DIM = 128
H_A = 8
H_B = 4
H_C = 4
DIFF_DIM = HEAD_DIM // 2
BAND_CHUNKS = 8
WINDOW_B = BAND_CHUNKS * CHUNK
BAND_KEYS = WINDOW_B + CHUNK
REL_MAX_PAST = 128
ROPE_THETA = 10000.0
EPS = 1e-6
NEG = -1e30
LOG2E = math.log2(math.e)

LANES = 128
MXU_DIM = 256
VMEM_LIMIT = 52 * 1024 * 1024

WA, WB, WC = H_A * HEAD_DIM, H_B * HEAD_DIM, H_C * HEAD_DIM
Q_WIDTH = WA + WB + WC
QB_BLK = H_A
QC_BLK = H_A + H_B
P_WIDTH = 3 * (WA + WB + WC)
TN = 1024


def _params(*sem):
    return pltpu.CompilerParams(dimension_semantics=sem, vmem_limit_bytes=VMEM_LIMIT)


def _rms_rows(x, g):
    ms = jnp.mean(x * x, axis=-1, keepdims=True)
    return x * lax.rsqrt(ms + EPS) * g


def _log_sigmoid(z):
    return jnp.minimum(z, 0.0) - jnp.log1p(jnp.exp(-jnp.abs(z)))


ROW_CHUNK = 256


N_ALIAS = 5


def _proj_kernel(*refs, tm, kc_transposed):
    (x_ref, g_ref, w_ref, wf_ref, bf_ref, gain_ref, cos_ref, sin_ref, g128_ref, g64_ref) = refs[:10]
    (q_ref, kbvb_ref, ka_ref, va_ref, kc_ref, vc_ref, lf_ref, h_ref, acc_ref) = refs[10 + N_ALIAS:]
    j = pl.program_id(1)

    @pl.when(j == 0)
    def _():
        h = _rms_rows(x_ref[...], g_ref[...]).astype(BF16)
        h_ref[...] = h
        f = jnp.dot(h, wf_ref[...], preferred_element_type=F32) + bf_ref[...]
        lf_ref[...] = _log_sigmoid(f)[:, :H_A]

    def finish(tile, c0, c1, mode, dst_ref, d0, transposed=False):
        for r0 in range(0, tm, ROW_CHUNK):
            rs = slice(r0, r0 + ROW_CHUNK)
            for c in range(c0, c1, MXU_DIM):
                cs = slice(c, c + MXU_DIM)
                x = acc_ref[rs, cs]
                if mode == "id":
                    y = x
                else:
                    gm, hd = (g128_ref, HEAD_DIM) if mode == "n128" else (g64_ref, DIFF_DIM)
                    ssq = jnp.dot((x * x).astype(BF16), gm[...], preferred_element_type=F32)
                    gain = gain_ref[:, tile * TN + c:tile * TN + c + MXU_DIM]
                    y = x * lax.rsqrt(ssq * (1.0 / hd) + EPS) * gain
                if mode == "n64r":
                    lane = lax.broadcasted_iota(jnp.int32, y.shape, 1)
                    first = (lane % DIFF_DIM) < (DIFF_DIM // 2)
                    cos = jnp.concatenate([cos_ref[rs, :]] * (MXU_DIM // LANES), axis=1)
                    sin = jnp.concatenate([sin_ref[rs, :]] * (MXU_DIM // LANES), axis=1)
                    partner = jnp.where(first, pltpu.roll(y, MXU_DIM - DIFF_DIM // 2, 1),
                                        pltpu.roll(y, DIFF_DIM // 2, 1))
                    y = y * cos + partner * sin
                ds = slice(d0 + c - c0, d0 + c - c0 + MXU_DIM)
                if transposed:
                    dst_ref[ds, rs] = y.T.astype(dst_ref.dtype)
                else:
                    dst_ref[rs, ds] = y.astype(dst_ref.dtype)

    plan = (
        ((0, WA, "n128", q_ref, 0),),
        ((0, WB, "n128", q_ref, WA), (WB, WB + WC, "n64r", q_ref, WA + WB)),
        ((0, WA, "n128", ka_ref, 0),),
        ((0, WA, "id", va_ref, 0),),
        ((0, WB, "n128", kbvb_ref, 0), (WB, 2 * WB, "id", kbvb_ref, WB)),
        ((0, WC, "n64r", kc_ref, 0, kc_transposed), (WC, 2 * WC, "id", vc_ref, 0)),
    )
    acc_ref[...] = jnp.dot(h_ref[...], w_ref[...], preferred_element_type=F32)
    for jj, pieces in enumerate(plan):
        @pl.when(j == jj)
        def _(jj=jj, pieces=pieces):
            for piece in pieces:
                finish(jj, *piece)


def _project(x, g_mix, prm, cos_t, sin_t, bufs, *, tm, layer, kc_rows=None):
    n, d = x.shape
    period = cos_t.shape[0] // tm
    n_tiles = P_WIDTH // TN
    depth = bufs[0].shape[0]
    stack = lambda width: jax.ShapeDtypeStruct((depth, n, width), F32)
    lay = lambda width: pl.BlockSpec((None, tm, width), lambda i, j: (layer, i, 0))
    const = lambda shape: pl.BlockSpec(shape, lambda i, j: (0,) * len(shape))
    if kc_rows is None:
        kc_shape, kc_spec = stack(WC), lay(WC)
    else:
        per = kc_rows // tm
        kc_shape = jax.ShapeDtypeStruct((depth, n // kc_rows, WC, kc_rows), F32)
        kc_spec = pl.BlockSpec((None, None, WC, tm), lambda i, j: (layer, i // per, 0, i % per))
    in_specs = [
        pl.BlockSpec((tm, d), lambda i, j: (i, 0)),
        const((1, d)),
        pl.BlockSpec((d, TN), lambda i, j: (0, j)),
        const((d, LANES)),
        const((1, LANES)),
        const((1, P_WIDTH)),
        pl.BlockSpec((tm, LANES), lambda i, j: (i % period, 0)),
        pl.BlockSpec((tm, LANES), lambda i, j: (i % period, 0)),
        const((MXU_DIM, MXU_DIM)),
        const((MXU_DIM, MXU_DIM)),
    ] + [pl.BlockSpec(memory_space=pl.ANY)] * N_ALIAS
    assert len(bufs) == N_ALIAS
    out = pl.pallas_call(
        functools.partial(_proj_kernel, tm=tm, kc_transposed=kc_rows is not None),
        grid=(n // tm, n_tiles),
        in_specs=in_specs,
        out_specs=[
            pl.BlockSpec((tm, Q_WIDTH), lambda i, j: (i, 0)),
            pl.BlockSpec((tm, 2 * WB), lambda i, j: (i, 0)),
            lay(WA), lay(WA), kc_spec, lay(WC), lay(H_A),
        ],
        out_shape=[
            jax.ShapeDtypeStruct((n, Q_WIDTH), BF16),
            jax.ShapeDtypeStruct((n, 2 * WB), F32),
            stack(WA), stack(WA), kc_shape, stack(WC), stack(H_A),
        ],
        input_output_aliases={10 + k: 2 + k for k in range(N_ALIAS)},
        scratch_shapes=[pltpu.VMEM((tm, d), BF16), pltpu.VMEM((tm, TN), F32)],
        compiler_params=_params("parallel", "arbitrary"),
        name="proj",
    )(x, g_mix, prm["w"], prm["wf"], prm["bf"], prm["gain"], cos_t, sin_t, prm["g128"], prm["g64"], *bufs)
    return out[0], out[1], tuple(out[2:])


def _cumsum_kernel(x_ref, o_ref):
    x = x_ref[...]
    n = x.shape[1]
    lane = lax.broadcasted_iota(jnp.int32, x.shape, 1)
    s = 1
    while s < n:
        x = x + jnp.where(lane >= s, pltpu.roll(x, s, 1), 0.0)
        s *= 2
    o_ref[...] = x


def _cumsum_lanes(x):
    r, n = x.shape
    rb = 8
    return pl.pallas_call(
        _cumsum_kernel,
        grid=(r // rb,),
        in_specs=[pl.BlockSpec((rb, n), lambda i: (i, 0))],
        out_specs=pl.BlockSpec((rb, n), lambda i: (i, 0)),
        out_shape=jax.ShapeDtypeStruct((r, n), F32),
        compiler_params=_params("parallel"),
        name="cumsum",
    )(x)


V_ROWS = HEAD_DIM + 16


def _flash_t(heads, mask_fn, *, tk, n_full):
    for _, _, _, m_ref, acc_ref, _ in heads:
        m_ref[...] = jnp.full(m_ref.shape, -jnp.inf, F32)
        acc_ref[...] = jnp.zeros(acc_ref.shape, F32)

    def scores(kb, slot):
        ks = pl.ds(pl.multiple_of(kb * tk, tk), tk)
        for q, kbf, _, _, _, s_ref in heads:
            s_ref[slot] = lax.dot_general(kbf[ks, :], q, (((1,), (1,)), ((), ())), preferred_element_type=F32)

    def softmax_pv(kb, slot, masked):
        for _, _, vt, m_ref, acc_ref, s_ref in heads:
            st = s_ref[slot]
            if masked:
                st = mask_fn(st)
            m_prev = m_ref[...]
            m_new = jnp.maximum(m_prev, jnp.max(st, axis=0, keepdims=True))
            m_ref[...] = m_new
            pt = jnp.exp2(st - m_new).astype(BF16)
            alpha = jnp.exp2(m_prev - m_new)
            acc_ref[...] = alpha * acc_ref[...] + jnp.dot(vt[kb], pt, preferred_element_type=F32)

    scores(0, 0)

    def pair(jp, carry):
        scores(2 * jp + 1, 1)
        softmax_pv(2 * jp, 0, False)
        scores(2 * jp + 2, 0)
        softmax_pv(2 * jp + 1, 1, False)
        return carry

    lax.fori_loop(0, n_full // 2, pair, 0)
    odd = n_full % 2 == 1

    @pl.when(odd)
    def _():
        scores(n_full, 1)
        softmax_pv(n_full - 1, 0, False)
        softmax_pv(n_full, 1, True)

    @pl.when(jnp.logical_not(odd))
    def _():
        softmax_pv(n_full, 0, True)


def _fill_vt(vt, v_ref, tk):
    ones_row = jnp.where(lax.broadcasted_iota(jnp.int32, (V_ROWS - HEAD_DIM, tk), 0) == 0, 1.0, 0.0)
    for kb in range(vt.shape[0]):
        vt[kb, :HEAD_DIM, :] = v_ref[kb * tk:(kb + 1) * tk, :].T.astype(BF16)
        vt[kb, HEAD_DIM:, :] = ones_row.astype(BF16)


def _split3(c):
    hi = c.astype(BF16).astype(F32)
    r1 = c - hi
    mid = r1.astype(BF16).astype(F32)
    lo = (r1 - mid).astype(BF16).astype(F32)
    return hi, mid, lo


def _decay_lanes(c_rep, key_side):
    hi, mid, lo = _split3(c_rep)
    lane = lax.broadcasted_iota(jnp.int32, c_rep.shape, 1)
    if key_side:
        parts = (-hi, -mid, -lo, 1.0, 1.0, 1.0)
    else:
        parts = (1.0, 1.0, 1.0, hi, mid, lo)
    vals = jnp.zeros(c_rep.shape, F32)
    for idx, part in enumerate(parts):
        vals = jnp.where(lane == idx, part, vals)
    return vals.astype(BF16)


HEAD_GROUP = 2


def _fox_kernel(q_ref, k_ref, v_ref, crow_ref, o_ref, kbf, vt, crep, m_ref, acc_ref, s_ref, *, tq):
    n_blocks = vt.shape[1]
    for g in range(HEAD_GROUP):
        gs = slice(g * HEAD_DIM, (g + 1) * HEAD_DIM)
        kbf[g, :, :HEAD_DIM] = k_ref[:, gs].astype(BF16)
        _fill_vt(vt.at[g], v_ref.at[:, gs], tq)
        for kb in range(n_blocks):
            rs = slice(kb * tq, (kb + 1) * tq)
            c_rep = jnp.broadcast_to(crow_ref[g, :, rs] * LOG2E, (LANES, tq)).T
            crep[g, rs, :] = c_rep
            kbf[g, rs, HEAD_DIM:] = _decay_lanes(c_rep, True)

    def mask_fn(st):
        kpos = lax.broadcasted_iota(jnp.int32, st.shape, 0)
        qpos = lax.broadcasted_iota(jnp.int32, st.shape, 1)
        return jnp.where(kpos <= qpos, st, NEG)

    def q_block(i, carry):
        rows = pl.ds(pl.multiple_of(i * tq, tq), tq)
        heads = []
        for g in range(HEAD_GROUP):
            gs = slice(g * HEAD_DIM, (g + 1) * HEAD_DIM)
            q = jnp.concatenate([q_ref[rows, gs], _decay_lanes(crep[g, rows, :], False)], axis=1)
            heads.append((q, kbf.at[g], vt.at[g], m_ref.at[g], acc_ref.at[g], s_ref.at[g]))
        _flash_t(heads, mask_fn, tk=tq, n_full=i)
        for g in range(HEAD_GROUP):
            acc = acc_ref[g]
            o_ref[rows, g * HEAD_DIM:(g + 1) * HEAD_DIM] = \
                (acc[:HEAD_DIM] / acc[HEAD_DIM:HEAD_DIM + 1]).T.astype(BF16)
        return carry

    lax.fori_loop(0, n_blocks, q_block, 0)


def _fox_prompt(q3, ka, va, c, *, layer, tq):
    b, t, _ = q3.shape
    crow = c.reshape(b, H_A, 1, t)
    gw = HEAD_GROUP * HEAD_DIM
    kv = lambda: pl.BlockSpec((None, None, t, gw), lambda bi, h: (layer, bi, 0, h))
    return pl.pallas_call(
        functools.partial(_fox_kernel, tq=tq),
        grid=(b, H_A // HEAD_GROUP),
        in_specs=[
            pl.BlockSpec((None, t, gw), lambda bi, h: (bi, 0, h)),
            kv(), kv(),
            pl.BlockSpec((None, HEAD_GROUP, 1, t), lambda bi, h: (bi, h, 0, 0)),
        ],
        out_specs=pl.BlockSpec((None, t, gw), lambda bi, h: (bi, 0, h)),
        out_shape=jax.ShapeDtypeStruct((b, t, WA), BF16),
        scratch_shapes=[
            pltpu.VMEM((HEAD_GROUP, t, 2 * HEAD_DIM), BF16),
            pltpu.VMEM((HEAD_GROUP, t // tq, V_ROWS, tq), BF16),
            pltpu.VMEM((HEAD_GROUP, t, LANES), F32),
            pltpu.VMEM((HEAD_GROUP, 1, tq), F32), pltpu.VMEM((HEAD_GROUP, V_ROWS, tq), F32),
            pltpu.VMEM((HEAD_GROUP, 2, tq, tq), F32),
        ],
        compiler_params=_params("parallel", "parallel"),
        name="fox_prompt",
    )(q3, ka, va, crow)


def _lambda(lq1_ref, lk1_ref, lq2_ref, lk2_ref, lam_init):
    a = jnp.sum(lq1_ref[...] * lk1_ref[...], axis=-1, keepdims=True)
    b = jnp.sum(lq2_ref[...] * lk2_ref[...], axis=-1, keepdims=True)
    return jnp.exp(a) - jnp.exp(b) + lam_init


def _stack_maps(q):
    lane = lax.broadcasted_iota(jnp.int32, q.shape, 1)
    lo = lane < DIFF_DIM
    zero = jnp.zeros_like(q)
    return jnp.concatenate([jnp.where(lo, q, zero), jnp.where(lo, zero, q)], axis=0)


def _subln(o, g, lam_init):
    return (_rms_rows(o, g) * (1.0 - lam_init)).astype(BF16)


def _diff_kernel(q_ref, kt_ref, v_ref, lq1_ref, lk1_ref, lq2_ref, lk2_ref, g_ref, o_ref,
                 kbf, vt, m_ref, acc_ref, s_ref, *, tq, lam_init):
    n_blocks = vt.shape[1]
    for g in range(HEAD_GROUP):
        gs = slice(g * HEAD_DIM, (g + 1) * HEAD_DIM)
        for kb in range(n_blocks):
            rs = slice(kb * tq, (kb + 1) * tq)
            kbf[g, rs, :] = kt_ref[gs, rs].T.astype(BF16)
        _fill_vt(vt.at[g], v_ref.at[:, gs], tq)
    lam = _lambda(lq1_ref, lk1_ref, lq2_ref, lk2_ref, lam_init)

    def mask_fn(st):
        kpos = lax.broadcasted_iota(jnp.int32, st.shape, 0)
        qpos = lax.broadcasted_iota(jnp.int32, st.shape, 1) % tq
        return jnp.where(kpos // CHUNK <= qpos // CHUNK, st, NEG)

    def q_block(i, carry):
        rows = pl.ds(pl.multiple_of(i * tq, tq), tq)
        heads = [(_stack_maps(q_ref[rows, g * HEAD_DIM:(g + 1) * HEAD_DIM]),
                  kbf.at[g], vt.at[g], m_ref.at[g], acc_ref.at[g], s_ref.at[g]) for g in range(HEAD_GROUP)]
        _flash_t(heads, mask_fn, tk=tq, n_full=i)
        for g in range(HEAD_GROUP):
            acc = acc_ref[g]
            o = acc[:HEAD_DIM] / acc[HEAD_DIM:HEAD_DIM + 1]
            o_ref[rows, g * HEAD_DIM:(g + 1) * HEAD_DIM] = \
                _subln((o[:, :tq] - lam * o[:, tq:]).T, g_ref[...], lam_init)
        return carry

    lax.fori_loop(0, n_blocks, q_block, 0)


def _lam_specs():
    return [pl.BlockSpec((1, DIFF_DIM), lambda *a: (0, 0)) for _ in range(4)] + \
           [pl.BlockSpec((1, LANES), lambda *a: (0, 0))]


def _diff_prompt(q3, kct, vc, lams, g_subln, *, layer, tq, lam_init):
    b, t, _ = q3.shape
    gw = HEAD_GROUP * HEAD_DIM
    return pl.pallas_call(
        functools.partial(_diff_kernel, tq=tq, lam_init=lam_init),
        grid=(b, H_C // HEAD_GROUP),
        in_specs=[
            pl.BlockSpec((None, t, gw), lambda bi, h: (bi, 0, QC_BLK // HEAD_GROUP + h)),
            pl.BlockSpec((None, None, gw, t), lambda bi, h: (layer, bi, h, 0)),
            pl.BlockSpec((None, None, t, gw), lambda bi, h: (layer, bi, 0, h)),
        ] + _lam_specs(),
        out_specs=pl.BlockSpec((None, t, gw), lambda bi, h: (bi, 0, h)),
        out_shape=jax.ShapeDtypeStruct((b, t, WC), BF16),
        scratch_shapes=[
            pltpu.VMEM((HEAD_GROUP, t, HEAD_DIM), BF16), pltpu.VMEM((HEAD_GROUP, t // tq, V_ROWS, tq), BF16),
            pltpu.VMEM((HEAD_GROUP, 1, 2 * tq), F32), pltpu.VMEM((HEAD_GROUP, V_ROWS, 2 * tq), F32),
            pltpu.VMEM((HEAD_GROUP, 2, tq, 2 * tq), F32),
        ],
        compiler_params=_params("parallel", "parallel"),
        name="diff_prompt",
    )(q3, kct, vc, *lams, g_subln)


BIAS_EXT = 5 * LANES


def _band_bias_tile(ext):
    x = jnp.broadcast_to(ext, (CHUNK, BIAS_EXT))
    return pltpu.roll(x, BIAS_EXT - (CHUNK - 1), 1, stride=1, stride_axis=0)[:, :BAND_KEYS]


def _band_chunk(q, k, v, bias, first_key_pos):
    s = lax.dot_general(q, k, (((1,), (1,)), ((), ())), preferred_element_type=F32) + bias
    kpos = first_key_pos + lax.broadcasted_iota(jnp.int32, s.shape, 1)
    s = jnp.where(kpos >= 0, s, NEG)
    m = jnp.max(s, axis=1, keepdims=True)
    p = jnp.exp2(s - m)
    l = jnp.sum(p, axis=1, keepdims=True)
    return (jnp.dot(p.astype(BF16), v, preferred_element_type=F32) / l).astype(BF16)


BAND_STEP = 256
BAND_WIN = BAND_STEP + WINDOW_B
BM_SPAN = BAND_STEP + LANES
BM_EXT = BM_SPAN + (BAND_WIN // LANES - 1) * LANES


def _band_biasmask(g_ref, bm_ref):
    n_kb = BAND_WIN // LANES
    for kb in range(n_kb):
        start = LANES * (n_kb - 1 - kb)
        x = jnp.broadcast_to(g_ref[:, start:start + BM_SPAN], (LANES, BM_SPAN))
        tile = pltpu.roll(x, BM_SPAN - (LANES - 1), 1, stride=1, stride_axis=0)[:, :BAND_STEP]
        k = kb * LANES + lax.broadcasted_iota(jnp.int32, (LANES, BAND_STEP), 0)
        first = (lax.broadcasted_iota(jnp.int32, (LANES, BAND_STEP), 1) // CHUNK) * CHUNK
        tile = jnp.where(k >= first, tile, NEG)
        bm_ref[kb * LANES:(kb + 1) * LANES, :] = jnp.where(k < first + BAND_KEYS, tile, NEG)


N_BAND_ALIAS = 2


def _band_kernel(*refs, t):
    q_ref, k_ref, v_ref, g_ref = refs[:4]
    o_ref, pbk_ref, pbv_ref, kpad, vt, bm_ref = refs[4 + N_BAND_ALIAS:]
    hg = pl.program_id(1)
    n_steps = t // BAND_STEP
    pad_chunks = WINDOW_B // BAND_STEP
    ones_row = jnp.where(lax.broadcasted_iota(jnp.int32, (V_ROWS - HEAD_DIM, BAND_STEP), 0) == 0, 1.0, 0.0)
    for g in range(HEAD_GROUP):
        gs = slice(g * HEAD_DIM, (g + 1) * HEAD_DIM)
        kpad[g, :WINDOW_B, :] = jnp.zeros((WINDOW_B, LANES), BF16)
        kpad[g, WINDOW_B:, :] = k_ref[:, gs].astype(BF16)
        for c in range(pad_chunks):
            vt[g, c] = jnp.zeros((V_ROWS, BAND_STEP), BF16)
        for kb in range(n_steps):
            vt[g, pad_chunks + kb, :HEAD_DIM, :] = v_ref[kb * BAND_STEP:(kb + 1) * BAND_STEP, gs].T.astype(BF16)
            vt[g, pad_chunks + kb, HEAD_DIM:, :] = ones_row.astype(BF16)
        _band_biasmask(g_ref.at[g], bm_ref.at[g])
        for hh in range(H_B // HEAD_GROUP):
            @pl.when(hg == hh)
            def _(head=hh * HEAD_GROUP + g, gs=gs):
                pbk_ref[pl.ds(head, WINDOW_B, stride=H_B), :] = k_ref[t - WINDOW_B:, gs]
                pbv_ref[pl.ds(head, WINDOW_B, stride=H_B), :] = v_ref[t - WINDOW_B:, gs]

    def step(i, carry):
        start = pl.multiple_of(i * BAND_STEP, BAND_STEP)
        rows = pl.ds(start, BAND_STEP)
        kpos = start - WINDOW_B + lax.broadcasted_iota(jnp.int32, (BAND_WIN, BAND_STEP), 0)
        for g in range(HEAD_GROUP):
            gs = slice(g * HEAD_DIM, (g + 1) * HEAD_DIM)
            st = lax.dot_general(kpad[g, pl.ds(start, BAND_WIN), :], q_ref[rows, gs], (((1,), (1,)), ((), ())),
                                 preferred_element_type=F32) + bm_ref[g]
            st = jnp.where(kpos >= 0, st, NEG)
            m = jnp.max(st, axis=0, keepdims=True)
            p = jnp.exp2(st - m).astype(BF16)
            acc = jnp.zeros((V_ROWS, BAND_STEP), F32)
            for c in range(BAND_WIN // BAND_STEP):
                acc += jnp.dot(vt[g, i + c], p[c * BAND_STEP:(c + 1) * BAND_STEP], preferred_element_type=F32)
            o_ref[rows, gs] = (acc[:HEAD_DIM] / acc[HEAD_DIM:HEAD_DIM + 1]).T.astype(BF16)
        return carry

    lax.fori_loop(0, n_steps, step, 0)


def _band_prompt(q3, kbvb, g, bufs, *, layer):
    b, t, _ = q3.shape
    assert len(bufs) == N_BAND_ALIAS
    gw = HEAD_GROUP * HEAD_DIM
    groups = H_B // HEAD_GROUP
    keep = jax.ShapeDtypeStruct(bufs[0].shape, F32)
    keep_spec = pl.BlockSpec((None, None, WINDOW_B * H_B, HEAD_DIM), lambda bi, hg: (layer, bi, 0, 0))
    out = pl.pallas_call(
        functools.partial(_band_kernel, t=t),
        grid=(b, groups),
        in_specs=[
            pl.BlockSpec((None, t, gw), lambda bi, hg: (bi, 0, QB_BLK // HEAD_GROUP + hg)),
            pl.BlockSpec((None, t, gw), lambda bi, hg: (bi, 0, hg)),
            pl.BlockSpec((None, t, gw), lambda bi, hg: (bi, 0, groups + hg)),
            pl.BlockSpec((HEAD_GROUP, 1, BM_EXT), lambda bi, hg: (hg, 0, 0)),
        ] + [pl.BlockSpec(memory_space=pl.ANY)] * N_BAND_ALIAS,
        out_specs=[pl.BlockSpec((None, t, gw), lambda bi, hg: (bi, 0, hg)), keep_spec, keep_spec],
        out_shape=[jax.ShapeDtypeStruct((b, t, WB), BF16), keep, keep],
        input_output_aliases={4 + k: 1 + k for k in range(N_BAND_ALIAS)},
        scratch_shapes=[pltpu.VMEM((HEAD_GROUP, t + WINDOW_B, LANES), BF16),
                        pltpu.VMEM((HEAD_GROUP, (t + WINDOW_B) // BAND_STEP, V_ROWS, BAND_STEP), BF16),
                        pltpu.VMEM((HEAD_GROUP, BAND_WIN, BAND_STEP), F32)],
        compiler_params=_params("parallel", "arbitrary"),
        name="band_prompt",
    )(q3, kbvb, kbvb, g, *bufs)
    return out[0], tuple(out[1:])


SAMPLE_KV = 2048


def _qk(q, k):
    return lax.dot_general(q, k, (((1,), (1,)), ((), ())), preferred_element_type=F32)


def _online_rows(s, v, m_ref, l_ref, acc_ref, h):
    m_prev = m_ref[h]
    m_new = jnp.maximum(m_prev, jnp.max(s, axis=1, keepdims=True))
    alpha = jnp.exp2(m_prev - m_new)
    p = jnp.exp2(s - m_new)
    l_ref[h] = alpha * l_ref[h] + jnp.sum(p, axis=1, keepdims=True)
    acc_ref[h] = alpha * acc_ref[h] + jnp.dot(p.astype(BF16), v, preferred_element_type=F32)
    m_ref[h] = m_new


def _reset_rows(m_ref, l_ref, acc_ref):
    m_ref[...] = jnp.full(m_ref.shape, -jnp.inf, F32)
    l_ref[...] = jnp.zeros(l_ref.shape, F32)
    acc_ref[...] = jnp.zeros(acc_ref.shape, F32)


def _fox_sample_kernel(q_ref, kn_ref, vn_ref, kc_ref, vc_ref, crow_ref, cnew_ref, ccol_ref, o_ref,
                       m_ref, l_ref, acc_ref):
    j = pl.program_id(1)
    n = q_ref.shape[0]

    @pl.when(j == 0)
    def _():
        _reset_rows(m_ref, l_ref, acc_ref)

    for h in range(H_A):
        hs = slice(h * HEAD_DIM, (h + 1) * HEAD_DIM)
        rows = pl.ds(h, kc_ref.shape[0] // H_A, stride=H_A)
        s = _qk(q_ref[:, hs], kc_ref[rows, :].astype(BF16)) + \
            (ccol_ref[h] - crow_ref[h:h + 1, :]) * LOG2E
        _online_rows(s, vc_ref[rows, :].astype(BF16), m_ref, l_ref, acc_ref, h)

    @pl.when(j == pl.num_programs(1) - 1)
    def _():
        row = lax.broadcasted_iota(jnp.int32, (n, n), 0)
        col = lax.broadcasted_iota(jnp.int32, (n, n), 1)
        for h in range(H_A):
            hs = slice(h * HEAD_DIM, (h + 1) * HEAD_DIM)
            s = _qk(q_ref[:, hs], kn_ref[:, hs].astype(BF16)) + \
                (ccol_ref[h] - cnew_ref[h:h + 1, :n]) * LOG2E
            _online_rows(jnp.where(col <= row, s, NEG), vn_ref[:, hs].astype(BF16), m_ref, l_ref, acc_ref, h)
            o_ref[:, hs] = (acc_ref[h] / l_ref[h]).astype(BF16)


def _sample_scratch(heads, rows):
    return [pltpu.VMEM((heads, rows, 1), F32), pltpu.VMEM((heads, rows, 1), F32),
            pltpu.VMEM((heads, rows, HEAD_DIM), F32)]


def _fox_sample(q3, kn, vn, ck, cv, c, *, layer):
    b, n, _ = q3.shape
    past = ck.shape[2] // H_A
    kv = _pick(past, SAMPLE_KV)
    new = lambda: pl.BlockSpec((None, None, n, WA), lambda bi, j: (layer, bi, 0, 0))
    cache = lambda: pl.BlockSpec((None, None, kv * H_A, HEAD_DIM), lambda bi, j: (layer, bi, j, 0))
    ccol = c[:, :, past:past + n].reshape(b, H_A, n, 1)
    return pl.pallas_call(
        _fox_sample_kernel,
        grid=(b, past // kv),
        in_specs=[
            pl.BlockSpec((None, n, WA), lambda bi, j: (bi, 0, 0)),
            new(), new(), cache(), cache(),
            pl.BlockSpec((None, H_A, kv), lambda bi, j: (bi, 0, j)),
            pl.BlockSpec((None, H_A, LANES), lambda bi, j: (bi, 0, past // LANES)),
            pl.BlockSpec((None, H_A, n, 1), lambda bi, j: (bi, 0, 0, 0)),
        ],
        out_specs=pl.BlockSpec((None, n, WA), lambda bi, j: (bi, 0, 0)),
        out_shape=jax.ShapeDtypeStruct((b, n, WA), BF16),
        scratch_shapes=_sample_scratch(H_A, n),
        compiler_params=_params("parallel", "arbitrary"),
        name="fox_sample",
    )(q3, kn, vn, ck, cv, c, c, ccol)


def _band_sample_kernel(*refs, first_key_pos):
    q_ref, kbvb_ref, kc_ref, vc_ref, ext_ref = refs[:5]
    o_ref, sbk_ref, sbv_ref = refs[5 + N_BAND_ALIAS:]
    n = q_ref.shape[0]
    rows = kc_ref.shape[0] // H_B
    keep = rows - n
    sbk_ref[:keep * H_B, :] = kc_ref[n * H_B:, :]
    sbv_ref[:keep * H_B, :] = vc_ref[n * H_B:, :]
    for h in range(H_B):
        hs = slice(h * HEAD_DIM, (h + 1) * HEAD_DIM)
        kn = kbvb_ref[:, hs]
        vn = kbvb_ref[:, WB + h * HEAD_DIM:WB + (h + 1) * HEAD_DIM]
        new_rows = pl.ds(keep * H_B + h, n, stride=H_B)
        sbk_ref[new_rows, :] = kn
        sbv_ref[new_rows, :] = vn
        cached = pl.ds(h, rows, stride=H_B)
        k = jnp.concatenate([kc_ref[cached, :], kn], axis=0).astype(BF16)
        v = jnp.concatenate([vc_ref[cached, :], vn], axis=0).astype(BF16)
        o_ref[:, hs] = _band_chunk(q_ref[:, hs], k, v, _band_bias_tile(ext_ref[h]), first_key_pos)


def _band_sample(q3, kbvb, ck, cv, ext, bufs, *, layer, first_key_pos):
    b, n, _ = q3.shape
    rows_h = ck.shape[2]
    assert len(bufs) == N_BAND_ALIAS
    roll_spec = lambda: pl.BlockSpec((None, None, rows_h, HEAD_DIM), lambda bi: (layer, bi, 0, 0))
    rolled = jax.ShapeDtypeStruct(ck.shape, F32)
    out = pl.pallas_call(
        functools.partial(_band_sample_kernel, first_key_pos=first_key_pos),
        grid=(b,),
        in_specs=[
            pl.BlockSpec((None, n, WB), lambda bi: (bi, 0, WA // WB)),
            pl.BlockSpec((None, n, 2 * WB), lambda bi: (bi, 0, 0)),
            roll_spec(), roll_spec(),
            pl.BlockSpec((H_B, 1, BIAS_EXT), lambda bi: (0, 0, 0)),
        ] + [pl.BlockSpec(memory_space=pl.ANY)] * N_BAND_ALIAS,
        out_specs=[pl.BlockSpec((None, n, WB), lambda bi: (bi, 0, 0)), roll_spec(), roll_spec()],
        out_shape=[jax.ShapeDtypeStruct((b, n, WB), BF16), rolled, rolled],
        input_output_aliases={5 + k: 1 + k for k in range(N_BAND_ALIAS)},
        compiler_params=_params("parallel"),
        name="band_sample",
    )(q3, kbvb, ck, cv, ext, *bufs)
    return out[0], tuple(out[1:])


def _diff_sample_kernel(q_ref, kn_ref, vn_ref, kt_ref, vc_ref, lq1_ref, lk1_ref, lq2_ref, lk2_ref,
                        g_ref, o_ref, m_ref, l_ref, acc_ref, *, lam_init):
    j = pl.program_id(1)
    n = q_ref.shape[0]

    @pl.when(j == 0)
    def _():
        _reset_rows(m_ref, l_ref, acc_ref)

    for h in range(H_C):
        q = q_ref[:, h * HEAD_DIM:(h + 1) * HEAD_DIM]
        s = jnp.concatenate(
            [jnp.dot(q[:, :DIFF_DIM], kt_ref[h, 0].astype(BF16), preferred_element_type=F32),
             jnp.dot(q[:, DIFF_DIM:], kt_ref[h, 1].astype(BF16), preferred_element_type=F32)], axis=0)
        rows = pl.ds(h, vc_ref.shape[0] // H_C, stride=H_C)
        _online_rows(s, vc_ref[rows, :].astype(BF16), m_ref, l_ref, acc_ref, h)

    @pl.when(j == pl.num_programs(1) - 1)
    def _():
        lam = _lambda(lq1_ref, lk1_ref, lq2_ref, lk2_ref, lam_init)
        for h in range(H_C):
            hs = slice(h * HEAD_DIM, (h + 1) * HEAD_DIM)
            s = _qk(_stack_maps(q_ref[:, hs]), kn_ref[:, hs].astype(BF16))
            _online_rows(s, vn_ref[:, hs].astype(BF16), m_ref, l_ref, acc_ref, h)
            o = acc_ref[h] / l_ref[h]
            o_ref[:, hs] = _subln(o[:n] - lam * o[n:], g_ref[...], lam_init)


def _diff_sample(q3, kn, vn, ckt, cv, lams, g_subln, *, layer, lam_init):
    b, n, _ = q3.shape
    past = cv.shape[2] // H_C
    kv = _pick(past, SAMPLE_KV)
    new = lambda: pl.BlockSpec((None, None, n, WC), lambda bi, j: (layer, bi, 0, 0))
    return pl.pallas_call(
        functools.partial(_diff_sample_kernel, lam_init=lam_init),
        grid=(b, past // kv),
        in_specs=[
            pl.BlockSpec((None, n, WC), lambda bi, j: (bi, 0, (WA + WB) // WC)),
            new(), new(),
            pl.BlockSpec((None, None, H_C, 2, DIFF_DIM, kv), lambda bi, j: (layer, bi, 0, 0, 0, j)),
            pl.BlockSpec((None, None, kv * H_C, HEAD_DIM), lambda bi, j: (layer, bi, j, 0)),
        ] + _lam_specs(),
        out_specs=pl.BlockSpec((None, n, WC), lambda bi, j: (bi, 0, 0)),
        out_shape=jax.ShapeDtypeStruct((b, n, WC), BF16),
        scratch_shapes=_sample_scratch(H_C, 2 * n),
        compiler_params=_params("parallel", "arbitrary"),
        name="diff_sample",
    )(q3, kn, vn, ckt, cv, *lams, g_subln)


def _merge_kernel(x_ref, oa_ref, ob_ref, oc_ref, w_ref, o_ref):
    o = jnp.concatenate([oa_ref[...], ob_ref[...], oc_ref[...]], axis=1)
    o_ref[...] = x_ref[...] + jnp.dot(o, w_ref[...], preferred_element_type=F32)


def _merge(x, oa, ob, oc, w, *, tm):
    n, d = x.shape
    mix = w.shape[0]
    return pl.pallas_call(
        _merge_kernel,
        grid=(n // tm,),
        in_specs=[
            pl.BlockSpec((tm, d), lambda i: (i, 0)),
            pl.BlockSpec((tm, oa.shape[1]), lambda i: (i, 0)),
            pl.BlockSpec((tm, ob.shape[1]), lambda i: (i, 0)),
            pl.BlockSpec((tm, oc.shape[1]), lambda i: (i, 0)),
            pl.BlockSpec((mix, d), lambda i: (0, 0)),
        ],
        out_specs=pl.BlockSpec((tm, d), lambda i: (i, 0)),
        out_shape=jax.ShapeDtypeStruct((n, d), F32),
        compiler_params=_params("parallel"),
        name="merge",
    )(x, oa, ob, oc, w)


def _mlp_kernel(x_ref, g_ref, wu_ref, wd_ref, o_ref, h_ref):
    @pl.when(pl.program_id(1) == 0)
    def _():
        x = x_ref[...]
        h_ref[...] = _rms_rows(x, g_ref[...]).astype(BF16)
        o_ref[...] = x

    u = jnp.dot(h_ref[...], wu_ref[...], preferred_element_type=F32)
    a = jnp.square(jnp.maximum(u, 0.0)).astype(BF16)
    o_ref[...] += jnp.dot(a, wd_ref[...], preferred_element_type=F32)


def _mlp(x, g, wu, wd, *, tm, tf):
    n, d = x.shape
    ff = wu.shape[1]
    return pl.pallas_call(
        _mlp_kernel,
        grid=(n // tm, ff // tf),
        in_specs=[
            pl.BlockSpec((tm, d), lambda i, f: (i, 0)),
            pl.BlockSpec((1, d), lambda i, f: (0, 0)),
            pl.BlockSpec((d, tf), lambda i, f: (0, f)),
            pl.BlockSpec((tf, d), lambda i, f: (f, 0)),
        ],
        out_specs=pl.BlockSpec((tm, d), lambda i, f: (i, 0)),
        out_shape=jax.ShapeDtypeStruct((n, d), F32),
        scratch_shapes=[pltpu.VMEM((tm, d), BF16)],
        compiler_params=_params("parallel", "arbitrary"),
        name="mlp",
    )(x, g, wu, wd)


def _split_cols(w):
    sizes = [WA] * 3 + [H_A] + [WB] * 3 + [WC] * 3
    out, c = [], 0
    for s in sizes:
        out.append(w[..., c:c + s])
        c += s
    return out


def _rope_tables(pos):
    half = DIFF_DIM // 2
    inv = ROPE_THETA ** (-jnp.arange(half, dtype=F32) * 2.0 / DIFF_DIM)
    ang = pos.astype(F32)[:, None] * inv[None, :]
    cos, sin = jnp.cos(ang), jnp.sin(ang)
    cos_t = jnp.tile(cos, (1, LANES // half))
    sin_t = jnp.tile(jnp.concatenate([-sin, sin], axis=1), (1, LANES // DIFF_DIM))
    return cos_t, sin_t


def _band_bias_ext(rel_table):
    far = WINDOW_B - REL_MAX_PAST + (CHUNK - 1)
    tab = rel_table.astype(F32) * LOG2E
    ext = jnp.concatenate(
        [jnp.broadcast_to(tab[:, -1:], (tab.shape[0], far)), tab[:, ::-1],
         jnp.broadcast_to(tab[:, :1], (tab.shape[0], BIAS_EXT - far - tab.shape[1]))], axis=1)
    return ext.reshape(tab.shape[0], 1, BIAS_EXT)


def _band_bias_reversed(ext):
    n_off = BAND_KEYS + CHUNK - 1
    top = CHUNK - 1 + LANES - 1 + BM_EXT - BM_SPAN
    rev = ext[:, :, :n_off][:, :, ::-1]
    lead = top - (n_off - 1)
    return jnp.pad(rev, ((0, 0), (0, 0), (lead, BM_EXT - lead - n_off)))


def _block_diag_ones(block):
    r = jnp.arange(MXU_DIM)
    return (r[:, None] // block == r[None, :] // block).astype(BF16)


def _pick(n, pref):
    return pref if n % pref == 0 else n


def kernel(x_prompt, x_sample, cache_a_k, cache_a_v, cache_a_logf, cache_b_k, cache_b_v, cache_c_k, cache_c_v, w_in, b_f, g_qa, g_ka, g_qb, g_kb, rel_bias, g_qc, g_kc, lam_q1, lam_k1, lam_q2, lam_k2, g_subln, w_out, g_mix, g_mlp, w_up, w_down):
    depth = w_in.shape[0]
    bp, t, d = x_prompt.shape
    bs, ns, _ = x_sample.shape
    past = cache_a_k.shape[2]
    b_rows = cache_b_k.shape[2]
    assert ns == CHUNK and b_rows == WINDOW_B and past % CHUNK == 0 and t % WINDOW_B == 0
    assert rel_bias.shape[-1] == REL_MAX_PAST + CHUNK
    n_p, n_s = bp * t, bs * ns

    tm_p = _pick(n_p, 512)
    tm_s = _pick(n_s, 512)
    tq = _pick(t, 512)

    cos_p, sin_p = _rope_tables(jnp.arange(t))
    cos_s, sin_s = _rope_tables(past + jnp.arange(ns))
    cos_s = jnp.tile(cos_s, (tm_s // ns, 1))
    sin_s = jnp.tile(sin_s, (tm_s // ns, 1))
    c_width = -(-(past + ns) // LANES) * LANES
    g128 = _block_diag_ones(HEAD_DIM)
    g64 = _block_diag_ones(DIFF_DIM)

    xp = x_prompt.reshape(n_p, d)
    xs = x_sample.reshape(n_s, d)
    zeros = lambda *shape: jnp.zeros(shape, F32)
    bufs_p = (zeros(depth, n_p, WA), zeros(depth, n_p, WA), zeros(depth, bp, WC, t), zeros(depth, n_p, WC),
              zeros(depth, n_p, H_A))
    bufs_s = (zeros(depth, n_s, WA), zeros(depth, n_s, WA), zeros(depth, n_s, WC), zeros(depth, n_s, WC),
              zeros(depth, n_s, H_A))
    band_p = (zeros(depth, bp, WINDOW_B * H_B, HEAD_DIM), zeros(depth, bp, WINDOW_B * H_B, HEAD_DIM))
    band_s = (zeros(depth, bs, b_rows * H_B, HEAD_DIM), zeros(depth, bs, b_rows * H_B, HEAD_DIM))
    ones = jnp.ones((HEAD_DIM,), F32)
    cache_c_kt = jnp.transpose(cache_c_k, (0, 1, 3, 4, 5, 2))
    cache_logf_rows = jnp.transpose(cache_a_logf.astype(F32), (0, 1, 3, 2))
    frames_by_head = lambda a: a.reshape(a.shape[0], a.shape[1], a.shape[2] * a.shape[3], HEAD_DIM)
    cak, cav, cbk, cbv, ccv = map(frames_by_head, (cache_a_k, cache_a_v, cache_b_k, cache_b_v, cache_c_v))

    for l in range(depth):
        lam_init = 0.8 - 0.6 * math.exp(-0.3 * l)
        qa, ka, va, fa, qb, kb, vb, qc, kc, vc = _split_cols(w_in[l])
        gqc = jnp.tile(g_qc[l], 2)
        gkc = jnp.tile(g_kc[l], 2)
        q_scale = HEAD_DIM ** -0.5 * LOG2E
        prm = dict(
            w=jnp.concatenate([qa, qb, qc, ka, va, kb, vb, kc, vc], axis=1).astype(BF16),
            wf=jnp.pad(fa, ((0, 0), (0, LANES - H_A))).astype(BF16),
            bf=jnp.pad(b_f[l], (0, LANES - H_A)).reshape(1, LANES),
            gain=jnp.concatenate(
                [jnp.tile(g_qa[l] * q_scale, H_A), jnp.tile(g_qb[l] * q_scale, H_B),
                 jnp.tile(gqc * (DIFF_DIM ** -0.5 * LOG2E), H_C),
                 jnp.tile(g_ka[l], H_A), jnp.tile(ones, H_A),
                 jnp.tile(g_kb[l], H_B), jnp.tile(ones, H_B),
                 jnp.tile(gkc, H_C), jnp.tile(ones, H_C)]).reshape(1, P_WIDTH),
            g128=g128, g64=g64)
        gmix = g_mix[l].reshape(1, d)
        gmlp = g_mlp[l].reshape(1, d)
        gsub = g_subln[l].reshape(1, HEAD_DIM)
        lams = [a[l].reshape(1, DIFF_DIM) for a in (lam_q1, lam_k1, lam_q2, lam_k2)]
        wo = w_out[l].astype(BF16)
        wu = w_up[l].astype(BF16)
        wd = w_down[l].astype(BF16)
        ext = _band_bias_ext(rel_bias[l])
        ext_rev = _band_bias_reversed(ext)

        qs, kbvb, bufs_p = _project(xp, gmix, prm, cos_p, sin_p, bufs_p, tm=tm_p, layer=l, kc_rows=t)
        ka_all, va_all, kct_all, vc_all, lf_all = bufs_p
        q3 = qs.reshape(bp, t, Q_WIDTH)
        logf_rows = jnp.transpose(lf_all[l].reshape(bp, t, H_A), (0, 2, 1)).reshape(bp * H_A, t)
        c_p = _cumsum_lanes(logf_rows)
        oa = _fox_prompt(q3, ka_all.reshape(depth, bp, t, WA), va_all.reshape(depth, bp, t, WA), c_p,
                         layer=l, tq=tq)
        ob, band_p = _band_prompt(q3, kbvb.reshape(bp, t, 2 * WB), ext_rev, band_p, layer=l)
        oc = _diff_prompt(q3, kct_all, vc_all.reshape(depth, bp, t, WC),
                          lams, gsub, layer=l, tq=tq, lam_init=lam_init)
        xp = _merge(xp, oa.reshape(n_p, -1), ob.reshape(n_p, -1), oc.reshape(n_p, -1), wo, tm=_pick(n_p, 512))
        xp = _mlp(xp, gmlp, wu, wd, tm=_pick(n_p, 512), tf=1024)

        qs, kbvb, bufs_s = _project(xs, gmix, prm, cos_s, sin_s, bufs_s, tm=tm_s, layer=l)
        ka_all, va_all, kc_all, vc_all, lf_all = bufs_s
        q3 = qs.reshape(bs, ns, Q_WIDTH)
        la_rows = jnp.transpose(lf_all[l].reshape(bs, ns, H_A), (0, 2, 1))
        logf_all = jnp.concatenate(
            [cache_logf_rows[l], la_rows, jnp.zeros((bs, H_A, c_width - past - ns), F32)], axis=2)
        c_s = _cumsum_lanes(logf_all.reshape(bs * H_A, c_width)).reshape(bs, H_A, c_width)
        oa = _fox_sample(q3, ka_all.reshape(depth, bs, ns, WA), va_all.reshape(depth, bs, ns, WA),
                         cak, cav, c_s, layer=l)
        ob, band_s = _band_sample(q3, kbvb.reshape(bs, ns, 2 * WB), cbk, cbv, ext, band_s,
                                  layer=l, first_key_pos=past - b_rows)
        oc = _diff_sample(q3, kc_all.reshape(depth, bs, ns, WC), vc_all.reshape(depth, bs, ns, WC),
                          cache_c_kt, ccv, lams, gsub, layer=l, lam_init=lam_init)
        xs = _merge(xs, oa.reshape(n_s, -1), ob.reshape(n_s, -1), oc.reshape(n_s, -1), wo, tm=_pick(n_s, 512))
        xs = _mlp(xs, gmlp, wu, wd, tm=_pick(n_s, 512), tf=1024)

    pak, pav, pckt, pcv, pal = bufs_p
    sak, sav, sck, scv, sal = bufs_s
    pck = jnp.transpose(pckt.reshape(depth, bp, H_C, 2, DIFF_DIM, t), (0, 1, 5, 2, 3, 4))
    return (xp.reshape(bp, t, d), xs.reshape(bs, ns, d),
            pak.reshape(depth, bp, t, H_A, HEAD_DIM), pav.reshape(depth, bp, t, H_A, HEAD_DIM),
            pal.reshape(depth, bp, t, H_A),
            band_p[0].reshape(depth, bp, WINDOW_B, H_B, HEAD_DIM), band_p[1].reshape(depth, bp, WINDOW_B, H_B, HEAD_DIM),
            pck, pcv.reshape(depth, bp, t, H_C, HEAD_DIM),
            sak.reshape(depth, bs, ns, H_A, HEAD_DIM), sav.reshape(depth, bs, ns, H_A, HEAD_DIM),
            sal.reshape(depth, bs, ns, H_A),
            band_s[0].reshape(depth, bs, b_rows, H_B, HEAD_DIM), band_s[1].reshape(depth, bs, b_rows, H_B, HEAD_DIM),
            sck.reshape(depth, bs, ns, H_C, 2, DIFF_DIM), scv.reshape(depth, bs, ns, H_C, HEAD_DIM))
```

```python
import functools
import math

import jax
import jax.numpy as jnp
from jax import lax
from jax.experimental import pallas as pl
from jax.experimental.pallas import tpu as pltpu

F32 = jnp.float32
BF16 = jnp.bfloat16

CHUNK = 64
HEAD_DIM = 128
H_A = 8
H_B = 4
H_C = 4
DIFF_DIM = HEAD_DIM // 2
BAND_CHUNKS = 8
WINDOW_B = BAND_CHUNKS * CHUNK
BAND_KEYS = WINDOW_B + CHUNK
REL_MAX_PAST = 128
ROPE_THETA = 10000.0
EPS = 1e-6
NEG = -1e30
LOG2E = math.log2(math.e)

LANES = 128
MXU_DIM = 256
VMEM_LIMIT = 52 * 1024 * 1024

WA, WB, WC = H_A * HEAD_DIM, H_B * HEAD_DIM, H_C * HEAD_DIM
Q_WIDTH = WA + WB + WC
QB_BLK = H_A
QC_BLK = H_A + H_B
P_WIDTH = 3 * (WA + WB + WC)
TN = 1024


def _params(*sem):
    return pltpu.CompilerParams(dimension_semantics=sem, vmem_limit_bytes=VMEM_LIMIT)


def _rms_rows(x, g):
    ms = jnp.mean(x * x, axis=-1, keepdims=True)
    return x * lax.rsqrt(ms + EPS) * g


def _log_sigmoid(z):
    return jnp.minimum(z, 0.0) - jnp.log1p(jnp.exp(-jnp.abs(z)))


ROW_CHUNK = 256


N_ALIAS = 5


def _proj_kernel(*refs, tm, kc_transposed):
    (x_ref, g_ref, w_ref, wf_ref, bf_ref, gain_ref, cos_ref, sin_ref, g128_ref, g64_ref) = refs[:10]
    (q_ref, kbvb_ref, ka_ref, va_ref, kc_ref, vc_ref, lf_ref, h_ref, acc_ref) = refs[10 + N_ALIAS:]
    j = pl.program_id(1)

    @pl.when(j == 0)
    def _():
        h = _rms_rows(x_ref[...], g_ref[...]).astype(BF16)
        h_ref[...] = h
        f = jnp.dot(h, wf_ref[...], preferred_element_type=F32) + bf_ref[...]
        lf_ref[...] = _log_sigmoid(f)[:, :H_A]

    def finish(tile, c0, c1, mode, dst_ref, d0, layout="rows"):
        rc = min(ROW_CHUNK, tm)
        for r0 in range(0, tm, rc):
            rs = slice(r0, r0 + rc)
            for c in range(c0, c1, MXU_DIM):
                cs = slice(c, c + MXU_DIM)
                x = acc_ref[rs, cs]
                if mode == "id":
                    y = x
                else:
                    gm, hd = (g128_ref, HEAD_DIM) if mode == "n128" else (g64_ref, DIFF_DIM)
                    ssq = jnp.dot((x * x).astype(BF16), gm[...], preferred_element_type=F32)
                    gain = gain_ref[:, tile * TN + c:tile * TN + c + MXU_DIM]
                    y = x * lax.rsqrt(ssq * (1.0 / hd) + EPS) * gain
                if mode == "n64r":
                    lane = lax.broadcasted_iota(jnp.int32, y.shape, 1)
                    first = (lane % DIFF_DIM) < (DIFF_DIM // 2)
                    cos = jnp.concatenate([cos_ref[rs, :]] * (MXU_DIM // LANES), axis=1)
                    sin = jnp.concatenate([sin_ref[rs, :]] * (MXU_DIM // LANES), axis=1)
                    partner = jnp.where(first, pltpu.roll(y, MXU_DIM - DIFF_DIM // 2, 1),
                                        pltpu.roll(y, DIFF_DIM // 2, 1))
                    y = y * cos + partner * sin
                ds = slice(d0 + c - c0, d0 + c - c0 + MXU_DIM)
                if layout == "transposed":
                    dst_ref[ds, rs] = y.T.astype(dst_ref.dtype)
                elif layout == "head_rows":
                    heads = (c1 - c0) // HEAD_DIM
                    for k in range(MXU_DIM // HEAD_DIM):
                        head = ds.start // HEAD_DIM + k
                        dst_ref[pl.ds(r0 * heads + head, rc, stride=heads), :] = \
                            y[:, k * HEAD_DIM:(k + 1) * HEAD_DIM]
                else:
                    dst_ref[rs, ds] = y.astype(dst_ref.dtype)

    plan = (
        ((0, WA, "n128", q_ref, 0),),
        ((0, WB, "n128", q_ref, WA), (WB, WB + WC, "n64r", q_ref, WA + WB)),
        ((0, WA, "n128", ka_ref, 0),),
        ((0, WA, "id", va_ref, 0),),
        ((0, WB, "n128", kbvb_ref, 0), (WB, 2 * WB, "id", kbvb_ref, WB)),
        ((0, WC, "n64r", kc_ref, 0, "transposed" if kc_transposed else "rows"),
         (WC, 2 * WC, "id", vc_ref, 0, "head_rows")),
    )
    acc_ref[...] = jnp.dot(h_ref[...], w_ref[...], preferred_element_type=F32)
    for jj, pieces in enumerate(plan):
        @pl.when(j == jj)
        def _(jj=jj, pieces=pieces):
            for piece in pieces:
                finish(jj, *piece)


def _project(x, g_mix, prm, cos_t, sin_t, bufs, *, tm, layer, kc_rows=None):
    n, d = x.shape
    period = cos_t.shape[0] // tm
    n_tiles = P_WIDTH // TN
    depth = bufs[0].shape[0]
    stack = lambda width: jax.ShapeDtypeStruct((depth, n, width), F32)
    lay = lambda width: pl.BlockSpec((None, tm, width), lambda i, j: (layer, i, 0))
    const = lambda shape: pl.BlockSpec(shape, lambda i, j: (0,) * len(shape))
    if kc_rows is None:
        kc_shape, kc_spec = stack(WC), lay(WC)
    else:
        per = kc_rows // tm
        kc_shape = jax.ShapeDtypeStruct((depth, n // kc_rows, WC, kc_rows), F32)
        kc_spec = pl.BlockSpec((None, None, WC, tm), lambda i, j: (layer, i // per, 0, i % per))
    in_specs = [
        pl.BlockSpec((tm, d), lambda i, j: (i, 0)),
        const((1, d)),
        pl.BlockSpec((d, TN), lambda i, j: (0, j)),
        const((d, LANES)),
        const((1, LANES)),
        const((1, P_WIDTH)),
        pl.BlockSpec((tm, LANES), lambda i, j: (i % period, 0)),
        pl.BlockSpec((tm, LANES), lambda i, j: (i % period, 0)),
        const((MXU_DIM, MXU_DIM)),
        const((MXU_DIM, MXU_DIM)),
    ] + [pl.BlockSpec(memory_space=pl.ANY)] * N_ALIAS
    assert len(bufs) == N_ALIAS
    out = pl.pallas_call(
        functools.partial(_proj_kernel, tm=tm, kc_transposed=kc_rows is not None),
        grid=(n // tm, n_tiles),
        in_specs=in_specs,
        out_specs=[
            pl.BlockSpec((tm, Q_WIDTH), lambda i, j: (i, 0)),
            pl.BlockSpec((tm, 2 * WB), lambda i, j: (i, 0)),
            lay(WA), lay(WA), kc_spec,
            pl.BlockSpec((None, tm * H_C, HEAD_DIM), lambda i, j: (layer, i, 0)), lay(H_A),
        ],
        out_shape=[
            jax.ShapeDtypeStruct((n, Q_WIDTH), BF16),
            jax.ShapeDtypeStruct((n, 2 * WB), F32),
            stack(WA), stack(WA), kc_shape,
            jax.ShapeDtypeStruct((depth, n * H_C, HEAD_DIM), F32), stack(H_A),
        ],
        input_output_aliases={10 + k: 2 + k for k in range(N_ALIAS)},
        scratch_shapes=[pltpu.VMEM((tm, d), BF16), pltpu.VMEM((tm, TN), F32)],
        compiler_params=_params("parallel", "arbitrary"),
        name="proj",
    )(x, g_mix, prm["w"], prm["wf"], prm["bf"], prm["gain"], cos_t, sin_t, prm["g128"], prm["g64"], *bufs)
    return out[0], out[1], tuple(out[2:])


def _cumsum_kernel(x_ref, o_ref):
    x = x_ref[...]
    n = x.shape[1]
    lane = lax.broadcasted_iota(jnp.int32, x.shape, 1)
    s = 1
    while s < n:
        x = x + jnp.where(lane >= s, pltpu.roll(x, s, 1), 0.0)
        s *= 2
    o_ref[...] = x


def _cumsum_lanes(x):
    r, n = x.shape
    rb = 8
    return pl.pallas_call(
        _cumsum_kernel,
        grid=(r // rb,),
        in_specs=[pl.BlockSpec((rb, n), lambda i: (i, 0))],
        out_specs=pl.BlockSpec((rb, n), lambda i: (i, 0)),
        out_shape=jax.ShapeDtypeStruct((r, n), F32),
        compiler_params=_params("parallel"),
        name="cumsum",
    )(x)


V_ROWS = HEAD_DIM + 16


def _flash_t(heads, mask_fn, *, tk, n_full):
    for _, _, _, m_ref, acc_ref, _ in heads:
        m_ref[...] = jnp.full(m_ref.shape, -jnp.inf, F32)
        acc_ref[...] = jnp.zeros(acc_ref.shape, F32)

    def scores(kb, slot):
        ks = pl.ds(pl.multiple_of(kb * tk, tk), tk)
        for q, kbf, _, _, _, s_ref in heads:
            s_ref[slot] = lax.dot_general(kbf[ks, :], q, (((1,), (1,)), ((), ())), preferred_element_type=F32)

    def softmax_pv(kb, slot, masked):
        for _, _, vt, m_ref, acc_ref, s_ref in heads:
            st = s_ref[slot]
            if masked:
                st = mask_fn(st)
            m_prev = m_ref[...]
            m_new = jnp.maximum(m_prev, jnp.max(st, axis=0, keepdims=True))
            m_ref[...] = m_new
            pt = jnp.exp2(st - m_new).astype(BF16)
            alpha = jnp.exp2(m_prev - m_new)
            acc_ref[...] = alpha * acc_ref[...] + jnp.dot(vt[kb], pt, preferred_element_type=F32)

    scores(0, 0)

    def pair(jp, carry):
        scores(2 * jp + 1, 1)
        softmax_pv(2 * jp, 0, False)
        scores(2 * jp + 2, 0)
        softmax_pv(2 * jp + 1, 1, False)
        return carry

    lax.fori_loop(0, n_full // 2, pair, 0)
    odd = n_full % 2 == 1

    @pl.when(odd)
    def _():
        scores(n_full, 1)
        softmax_pv(n_full - 1, 0, False)
        softmax_pv(n_full, 1, True)

    @pl.when(jnp.logical_not(odd))
    def _():
        softmax_pv(n_full, 0, True)


def _fill_vt(vt, v_ref, tk, head=0, heads=1):
    ones_row = jnp.where(lax.broadcasted_iota(jnp.int32, (V_ROWS - HEAD_DIM, tk), 0) == 0, 1.0, 0.0)
    for kb in range(vt.shape[0]):
        rows = pl.ds(kb * tk * heads + head, tk, stride=heads) if heads > 1 else slice(kb * tk, (kb + 1) * tk)
        vt[kb, :HEAD_DIM, :] = v_ref[rows, :].T.astype(BF16)
        vt[kb, HEAD_DIM:, :] = ones_row.astype(BF16)


def _split3(c):
    hi = c.astype(BF16).astype(F32)
    r1 = c - hi
    mid = r1.astype(BF16).astype(F32)
    lo = (r1 - mid).astype(BF16).astype(F32)
    return hi, mid, lo


def _decay_lanes(c_rep, key_side):
    hi, mid, lo = _split3(c_rep)
    lane = lax.broadcasted_iota(jnp.int32, c_rep.shape, 1)
    if key_side:
        parts = (-hi, -mid, -lo, 1.0, 1.0, 1.0)
    else:
        parts = (1.0, 1.0, 1.0, hi, mid, lo)
    vals = jnp.zeros(c_rep.shape, F32)
    for idx, part in enumerate(parts):
        vals = jnp.where(lane == idx, part, vals)
    return vals.astype(BF16)


HEAD_GROUP = 2


def _fox_kernel(q_ref, k_ref, v_ref, crow_ref, o_ref, kbf, vt, crep, m_ref, acc_ref, s_ref, *, tq):
    n_blocks = vt.shape[1]
    for g in range(HEAD_GROUP):
        gs = slice(g * HEAD_DIM, (g + 1) * HEAD_DIM)
        kbf[g, :, :HEAD_DIM] = k_ref[:, gs].astype(BF16)
        _fill_vt(vt.at[g], v_ref.at[:, gs], tq)
        for kb in range(n_blocks):
            rs = slice(kb * tq, (kb + 1) * tq)
            c_rep = jnp.broadcast_to(crow_ref[g, :, rs] * LOG2E, (LANES, tq)).T
            crep[g, rs, :] = c_rep
            kbf[g, rs, HEAD_DIM:] = _decay_lanes(c_rep, True)

    def mask_fn(st):
        kpos = lax.broadcasted_iota(jnp.int32, st.shape, 0)
        qpos = lax.broadcasted_iota(jnp.int32, st.shape, 1)
        return jnp.where(kpos <= qpos, st, NEG)

    def q_block(i, carry):
        rows = pl.ds(pl.multiple_of(i * tq, tq), tq)
        heads = []
        for g in range(HEAD_GROUP):
            gs = slice(g * HEAD_DIM, (g + 1) * HEAD_DIM)
            q = jnp.concatenate([q_ref[rows, gs], _decay_lanes(crep[g, rows, :], False)], axis=1)
            heads.append((q, kbf.at[g], vt.at[g], m_ref.at[g], acc_ref.at[g], s_ref.at[g]))
        _flash_t(heads, mask_fn, tk=tq, n_full=i)
        for g in range(HEAD_GROUP):
            acc = acc_ref[g]
            o_ref[rows, g * HEAD_DIM:(g + 1) * HEAD_DIM] = \
                (acc[:HEAD_DIM] / acc[HEAD_DIM:HEAD_DIM + 1]).T.astype(BF16)
        return carry

    lax.fori_loop(0, n_blocks, q_block, 0)


def _fox_prompt(q3, ka, va, c, *, layer, tq):
    b, t, _ = q3.shape
    crow = c.reshape(b, H_A, 1, t)
    gw = HEAD_GROUP * HEAD_DIM
    kv = lambda: pl.BlockSpec((None, None, t, gw), lambda bi, h: (layer, bi, 0, h))
    return pl.pallas_call(
        functools.partial(_fox_kernel, tq=tq),
        grid=(b, H_A // HEAD_GROUP),
        in_specs=[
            pl.BlockSpec((None, t, gw), lambda bi, h: (bi, 0, h)),
            kv(), kv(),
            pl.BlockSpec((None, HEAD_GROUP, 1, t), lambda bi, h: (bi, h, 0, 0)),
        ],
        out_specs=pl.BlockSpec((None, t, gw), lambda bi, h: (bi, 0, h)),
        out_shape=jax.ShapeDtypeStruct((b, t, WA), BF16),
        scratch_shapes=[
            pltpu.VMEM((HEAD_GROUP, t, 2 * HEAD_DIM), BF16),
            pltpu.VMEM((HEAD_GROUP, t // tq, V_ROWS, tq), BF16),
            pltpu.VMEM((HEAD_GROUP, t, LANES), F32),
            pltpu.VMEM((HEAD_GROUP, 1, tq), F32), pltpu.VMEM((HEAD_GROUP, V_ROWS, tq), F32),
            pltpu.VMEM((HEAD_GROUP, 2, tq, tq), F32),
        ],
        compiler_params=_params("parallel", "parallel"),
        name="fox_prompt",
    )(q3, ka, va, crow)


def _lambda(lq1_ref, lk1_ref, lq2_ref, lk2_ref, lam_init):
    a = jnp.sum(lq1_ref[...] * lk1_ref[...], axis=-1, keepdims=True)
    b = jnp.sum(lq2_ref[...] * lk2_ref[...], axis=-1, keepdims=True)
    return jnp.exp(a) - jnp.exp(b) + lam_init


def _stack_maps(q):
    lane = lax.broadcasted_iota(jnp.int32, q.shape, 1)
    lo = lane < DIFF_DIM
    zero = jnp.zeros_like(q)
    return jnp.concatenate([jnp.where(lo, q, zero), jnp.where(lo, zero, q)], axis=0)


def _subln(o, g, lam_init):
    return (_rms_rows(o, g) * (1.0 - lam_init)).astype(BF16)


def _diff_kernel(q_ref, kt_ref, v_ref, lq1_ref, lk1_ref, lq2_ref, lk2_ref, g_ref, o_ref,
                 kbf, vt, m_ref, acc_ref, s_ref, *, tq, lam_init):
    n_blocks = vt.shape[1]
    for g in range(HEAD_GROUP):
        gs = slice(g * HEAD_DIM, (g + 1) * HEAD_DIM)
        for kb in range(n_blocks):
            rs = slice(kb * tq, (kb + 1) * tq)
            kbf[g, rs, :] = kt_ref[gs, rs].T.astype(BF16)
        for hh in range(H_C // HEAD_GROUP):
            @pl.when(pl.program_id(1) == hh)
            def _(g=g, head=hh * HEAD_GROUP + g):
                _fill_vt(vt.at[g], v_ref, tq, head=head, heads=H_C)
    lam = _lambda(lq1_ref, lk1_ref, lq2_ref, lk2_ref, lam_init)

    def mask_fn(st):
        kpos = lax.broadcasted_iota(jnp.int32, st.shape, 0)
        qpos = lax.broadcasted_iota(jnp.int32, st.shape, 1) % tq
        return jnp.where(kpos // CHUNK <= qpos // CHUNK, st, NEG)

    def q_block(i, carry):
        rows = pl.ds(pl.multiple_of(i * tq, tq), tq)
        heads = [(_stack_maps(q_ref[rows, g * HEAD_DIM:(g + 1) * HEAD_DIM]),
                  kbf.at[g], vt.at[g], m_ref.at[g], acc_ref.at[g], s_ref.at[g]) for g in range(HEAD_GROUP)]
        _flash_t(heads, mask_fn, tk=tq, n_full=i)
        for g in range(HEAD_GROUP):
            acc = acc_ref[g]
            o = acc[:HEAD_DIM] / acc[HEAD_DIM:HEAD_DIM + 1]
            o_ref[rows, g * HEAD_DIM:(g + 1) * HEAD_DIM] = \
                _subln((o[:, :tq] - lam * o[:, tq:]).T, g_ref[...], lam_init)
        return carry

    lax.fori_loop(0, n_blocks, q_block, 0)


def _lam_specs():
    return [pl.BlockSpec((1, DIFF_DIM), lambda *a: (0, 0)) for _ in range(4)] + \
           [pl.BlockSpec((1, LANES), lambda *a: (0, 0))]


def _diff_prompt(q3, kct, vc, lams, g_subln, *, layer, tq, lam_init):
    b, t, _ = q3.shape
    gw = HEAD_GROUP * HEAD_DIM
    return pl.pallas_call(
        functools.partial(_diff_kernel, tq=tq, lam_init=lam_init),
        grid=(b, H_C // HEAD_GROUP),
        in_specs=[
            pl.BlockSpec((None, t, gw), lambda bi, h: (bi, 0, QC_BLK // HEAD_GROUP + h)),
            pl.BlockSpec((None, None, gw, t), lambda bi, h: (layer, bi, h, 0)),
            pl.BlockSpec((None, None, t * H_C, HEAD_DIM), lambda bi, h: (layer, bi, 0, 0)),
        ] + _lam_specs(),
        out_specs=pl.BlockSpec((None, t, gw), lambda bi, h: (bi, 0, h)),
        out_shape=jax.ShapeDtypeStruct((b, t, WC), BF16),
        scratch_shapes=[
            pltpu.VMEM((HEAD_GROUP, t, HEAD_DIM), BF16), pltpu.VMEM((HEAD_GROUP, t // tq, V_ROWS, tq), BF16),
            pltpu.VMEM((HEAD_GROUP, 1, 2 * tq), F32), pltpu.VMEM((HEAD_GROUP, V_ROWS, 2 * tq), F32),
            pltpu.VMEM((HEAD_GROUP, 2, tq, 2 * tq), F32),
        ],
        compiler_params=_params("parallel", "parallel"),
        name="diff_prompt",
    )(q3, kct, vc, *lams, g_subln)


BIAS_EXT = 5 * LANES


def _band_bias_tile(ext):
    x = jnp.broadcast_to(ext, (CHUNK, BIAS_EXT))
    return pltpu.roll(x, BIAS_EXT - (CHUNK - 1), 1, stride=1, stride_axis=0)[:, :BAND_KEYS]


def _band_chunk(q, k, v, bias, first_key_pos):
    s = lax.dot_general(q, k, (((1,), (1,)), ((), ())), preferred_element_type=F32) + bias
    kpos = first_key_pos + lax.broadcasted_iota(jnp.int32, s.shape, 1)
    s = jnp.where(kpos >= 0, s, NEG)
    m = jnp.max(s, axis=1, keepdims=True)
    p = jnp.exp2(s - m)
    l = jnp.sum(p, axis=1, keepdims=True)
    return (jnp.dot(p.astype(BF16), v, preferred_element_type=F32) / l).astype(BF16)


BAND_STEP = 256
BAND_WIN = BAND_STEP + WINDOW_B
BM_SPAN = BAND_STEP + LANES
BM_EXT = BM_SPAN + (BAND_WIN // LANES - 1) * LANES


def _band_biasmask(g_ref, bm_ref):
    n_kb = BAND_WIN // LANES
    for kb in range(n_kb):
        start = LANES * (n_kb - 1 - kb)
        x = jnp.broadcast_to(g_ref[:, start:start + BM_SPAN], (LANES, BM_SPAN))
        tile = pltpu.roll(x, BM_SPAN - (LANES - 1), 1, stride=1, stride_axis=0)[:, :BAND_STEP]
        k = kb * LANES + lax.broadcasted_iota(jnp.int32, (LANES, BAND_STEP), 0)
        first = (lax.broadcasted_iota(jnp.int32, (LANES, BAND_STEP), 1) // CHUNK) * CHUNK
        tile = jnp.where(k >= first, tile, NEG)
        bm_ref[kb * LANES:(kb + 1) * LANES, :] = jnp.where(k < first + BAND_KEYS, tile, NEG)


N_BAND_ALIAS = 2


def _band_kernel(*refs, t):
    q_ref, k_ref, v_ref, g_ref = refs[:4]
    o_ref, pbk_ref, pbv_ref, kpad, vt, bm_ref = refs[4 + N_BAND_ALIAS:]
    hg = pl.program_id(1)
    n_steps = t // BAND_STEP
    pad_chunks = WINDOW_B // BAND_STEP
    ones_row = jnp.where(lax.broadcasted_iota(jnp.int32, (V_ROWS - HEAD_DIM, BAND_STEP), 0) == 0, 1.0, 0.0)
    for g in range(HEAD_GROUP):
        gs = slice(g * HEAD_DIM, (g + 1) * HEAD_DIM)
        kpad[g, :WINDOW_B, :] = jnp.zeros((WINDOW_B, LANES), BF16)
        kpad[g, WINDOW_B:, :] = k_ref[:, gs].astype(BF16)
        for c in range(pad_chunks):
            vt[g, c] = jnp.zeros((V_ROWS, BAND_STEP), BF16)
        for kb in range(n_steps):
            vt[g, pad_chunks + kb, :HEAD_DIM, :] = v_ref[kb * BAND_STEP:(kb + 1) * BAND_STEP, gs].T.astype(BF16)
            vt[g, pad_chunks + kb, HEAD_DIM:, :] = ones_row.astype(BF16)
        _band_biasmask(g_ref.at[g], bm_ref.at[g])
        for hh in range(H_B // HEAD_GROUP):
            @pl.when(hg == hh)
            def _(head=hh * HEAD_GROUP + g, gs=gs):
                pbk_ref[pl.ds(head, WINDOW_B, stride=H_B), :] = k_ref[t - WINDOW_B:, gs]
                pbv_ref[pl.ds(head, WINDOW_B, stride=H_B), :] = v_ref[t - WINDOW_B:, gs]

    def step(i, carry):
        start = pl.multiple_of(i * BAND_STEP, BAND_STEP)
        rows = pl.ds(start, BAND_STEP)
        kpos = start - WINDOW_B + lax.broadcasted_iota(jnp.int32, (BAND_WIN, BAND_STEP), 0)
        for g in range(HEAD_GROUP):
            gs = slice(g * HEAD_DIM, (g + 1) * HEAD_DIM)
            st = lax.dot_general(kpad[g, pl.ds(start, BAND_WIN), :], q_ref[rows, gs], (((1,), (1,)), ((), ())),
                                 preferred_element_type=F32) + bm_ref[g]
            st = jnp.where(kpos >= 0, st, NEG)
            m = jnp.max(st, axis=0, keepdims=True)
            p = jnp.exp2(st - m).astype(BF16)
            acc = jnp.zeros((V_ROWS, BAND_STEP), F32)
            for c in range(BAND_WIN // BAND_STEP):
                acc += jnp.dot(vt[g, i + c], p[c * BAND_STEP:(c + 1) * BAND_STEP], preferred_element_type=F32)
            o_ref[rows, gs] = (acc[:HEAD_DIM] / acc[HEAD_DIM:HEAD_DIM + 1]).T.astype(BF16)
        return carry

    lax.fori_loop(0, n_steps, step, 0, unroll=2)


def _band_prompt(q3, kbvb, g, bufs, *, layer):
    b, t, _ = q3.shape
    assert len(bufs) == N_BAND_ALIAS
    gw = HEAD_GROUP * HEAD_DIM
    groups = H_B // HEAD_GROUP
    keep = jax.ShapeDtypeStruct(bufs[0].shape, F32)
    keep_spec = pl.BlockSpec((None, None, WINDOW_B * H_B, HEAD_DIM), lambda bi, hg: (layer, bi, 0, 0))
    out = pl.pallas_call(
        functools.partial(_band_kernel, t=t),
        grid=(b, groups),
        in_specs=[
            pl.BlockSpec((None, t, gw), lambda bi, hg: (bi, 0, QB_BLK // HEAD_GROUP + hg)),
            pl.BlockSpec((None, t, gw), lambda bi, hg: (bi, 0, hg)),
            pl.BlockSpec((None, t, gw), lambda bi, hg: (bi, 0, groups + hg)),
            pl.BlockSpec((HEAD_GROUP, 1, BM_EXT), lambda bi, hg: (hg, 0, 0)),
        ] + [pl.BlockSpec(memory_space=pl.ANY)] * N_BAND_ALIAS,
        out_specs=[pl.BlockSpec((None, t, gw), lambda bi, hg: (bi, 0, hg)), keep_spec, keep_spec],
        out_shape=[jax.ShapeDtypeStruct((b, t, WB), BF16), keep, keep],
        input_output_aliases={4 + k: 1 + k for k in range(N_BAND_ALIAS)},
        scratch_shapes=[pltpu.VMEM((HEAD_GROUP, t + WINDOW_B, LANES), BF16),
                        pltpu.VMEM((HEAD_GROUP, (t + WINDOW_B) // BAND_STEP, V_ROWS, BAND_STEP), BF16),
                        pltpu.VMEM((HEAD_GROUP, BAND_WIN, BAND_STEP), F32)],
        compiler_params=_params("parallel", "arbitrary"),
        name="band_prompt",
    )(q3, kbvb, kbvb, g, *bufs)
    return out[0], tuple(out[1:])


SAMPLE_KV = 2048


def _qk(q, k):
    return lax.dot_general(q, k, (((1,), (1,)), ((), ())), preferred_element_type=F32)


def _online_rows(s, v, m_ref, l_ref, acc_ref, h):
    m_prev = m_ref[h]
    m_new = jnp.maximum(m_prev, jnp.max(s, axis=1, keepdims=True))
    alpha = jnp.exp2(m_prev - m_new)
    p = jnp.exp2(s - m_new)
    l_ref[h] = alpha * l_ref[h] + jnp.sum(p, axis=1, keepdims=True)
    acc_ref[h] = alpha * acc_ref[h] + jnp.dot(p.astype(BF16), v, preferred_element_type=F32)
    m_ref[h] = m_new


def _reset_rows(m_ref, l_ref, acc_ref):
    m_ref[...] = jnp.full(m_ref.shape, -jnp.inf, F32)
    l_ref[...] = jnp.zeros(l_ref.shape, F32)
    acc_ref[...] = jnp.zeros(acc_ref.shape, F32)


def _fox_sample_kernel(q_ref, kn_ref, vn_ref, kc_ref, vc_ref, crow_ref, cnew_ref, ccol_ref, o_ref,
                       m_ref, l_ref, acc_ref):
    j = pl.program_id(1)
    n = q_ref.shape[0]

    @pl.when(j == 0)
    def _():
        _reset_rows(m_ref, l_ref, acc_ref)

    for h in range(H_A):
        hs = slice(h * HEAD_DIM, (h + 1) * HEAD_DIM)
        rows = pl.ds(h, kc_ref.shape[0] // H_A, stride=H_A)
        s = _qk(q_ref[:, hs], kc_ref[rows, :].astype(BF16)) + \
            (ccol_ref[h] - crow_ref[h:h + 1, :]) * LOG2E
        _online_rows(s, vc_ref[rows, :].astype(BF16), m_ref, l_ref, acc_ref, h)

    @pl.when(j == pl.num_programs(1) - 1)
    def _():
        row = lax.broadcasted_iota(jnp.int32, (n, n), 0)
        col = lax.broadcasted_iota(jnp.int32, (n, n), 1)
        for h in range(H_A):
            hs = slice(h * HEAD_DIM, (h + 1) * HEAD_DIM)
            s = _qk(q_ref[:, hs], kn_ref[:, hs].astype(BF16)) + \
                (ccol_ref[h] - cnew_ref[h:h + 1, :n]) * LOG2E
            _online_rows(jnp.where(col <= row, s, NEG), vn_ref[:, hs].astype(BF16), m_ref, l_ref, acc_ref, h)
            o_ref[:, hs] = (acc_ref[h] / l_ref[h]).astype(BF16)


def _sample_scratch(heads, rows):
    return [pltpu.VMEM((heads, rows, 1), F32), pltpu.VMEM((heads, rows, 1), F32),
            pltpu.VMEM((heads, rows, HEAD_DIM), F32)]


def _fox_sample(q3, kn, vn, ck, cv, c, *, layer):
    b, n, _ = q3.shape
    past = ck.shape[2] // H_A
    kv = _pick(past, SAMPLE_KV)
    new = lambda: pl.BlockSpec((None, None, n, WA), lambda bi, j: (layer, bi, 0, 0))
    cache = lambda: pl.BlockSpec((None, None, kv * H_A, HEAD_DIM), lambda bi, j: (layer, bi, j, 0))
    ccol = c[:, :, past:past + n].reshape(b, H_A, n, 1)
    return pl.pallas_call(
        _fox_sample_kernel,
        grid=(b, past // kv),
        in_specs=[
            pl.BlockSpec((None, n, WA), lambda bi, j: (bi, 0, 0)),
            new(), new(), cache(), cache(),
            pl.BlockSpec((None, H_A, kv), lambda bi, j: (bi, 0, j)),
            pl.BlockSpec((None, H_A, LANES), lambda bi, j: (bi, 0, past // LANES)),
            pl.BlockSpec((None, H_A, n, 1), lambda bi, j: (bi, 0, 0, 0)),
        ],
        out_specs=pl.BlockSpec((None, n, WA), lambda bi, j: (bi, 0, 0)),
        out_shape=jax.ShapeDtypeStruct((b, n, WA), BF16),
        scratch_shapes=_sample_scratch(H_A, n),
        compiler_params=_params("parallel", "arbitrary"),
        name="fox_sample",
    )(q3, kn, vn, ck, cv, c, c, ccol)


def _band_sample_kernel(*refs, first_key_pos):
    q_ref, kbvb_ref, kc_ref, vc_ref, ext_ref = refs[:5]
    o_ref, sbk_ref, sbv_ref = refs[5 + N_BAND_ALIAS:]
    n = q_ref.shape[0]
    rows = kc_ref.shape[0] // H_B
    keep = rows - n
    sbk_ref[:keep * H_B, :] = kc_ref[n * H_B:, :]
    sbv_ref[:keep * H_B, :] = vc_ref[n * H_B:, :]
    for h in range(H_B):
        hs = slice(h * HEAD_DIM, (h + 1) * HEAD_DIM)
        kn = kbvb_ref[:, hs]
        vn = kbvb_ref[:, WB + h * HEAD_DIM:WB + (h + 1) * HEAD_DIM]
        new_rows = pl.ds(keep * H_B + h, n, stride=H_B)
        sbk_ref[new_rows, :] = kn
        sbv_ref[new_rows, :] = vn
        cached = pl.ds(h, rows, stride=H_B)
        k = jnp.concatenate([kc_ref[cached, :], kn], axis=0).astype(BF16)
        v = jnp.concatenate([vc_ref[cached, :], vn], axis=0).astype(BF16)
        o_ref[:, hs] = _band_chunk(q_ref[:, hs], k, v, _band_bias_tile(ext_ref[h]), first_key_pos)


def _band_sample(q3, kbvb, ck, cv, ext, bufs, *, layer, first_key_pos):
    b, n, _ = q3.shape
    rows_h = ck.shape[2]
    assert len(bufs) == N_BAND_ALIAS
    roll_spec = lambda: pl.BlockSpec((None, None, rows_h, HEAD_DIM), lambda bi: (layer, bi, 0, 0))
    rolled = jax.ShapeDtypeStruct(ck.shape, F32)
    out = pl.pallas_call(
        functools.partial(_band_sample_kernel, first_key_pos=first_key_pos),
        grid=(b,),
        in_specs=[
            pl.BlockSpec((None, n, WB), lambda bi: (bi, 0, WA // WB)),
            pl.BlockSpec((None, n, 2 * WB), lambda bi: (bi, 0, 0)),
            roll_spec(), roll_spec(),
            pl.BlockSpec((H_B, 1, BIAS_EXT), lambda bi: (0, 0, 0)),
        ] + [pl.BlockSpec(memory_space=pl.ANY)] * N_BAND_ALIAS,
        out_specs=[pl.BlockSpec((None, n, WB), lambda bi: (bi, 0, 0)), roll_spec(), roll_spec()],
        out_shape=[jax.ShapeDtypeStruct((b, n, WB), BF16), rolled, rolled],
        input_output_aliases={5 + k: 1 + k for k in range(N_BAND_ALIAS)},
        compiler_params=_params("parallel"),
        name="band_sample",
    )(q3, kbvb, ck, cv, ext, *bufs)
    return out[0], tuple(out[1:])


def _diff_sample_kernel(q_ref, kn_ref, vn_ref, kt_ref, vc_ref, lq1_ref, lk1_ref, lq2_ref, lk2_ref,
                        g_ref, o_ref, m_ref, l_ref, acc_ref, *, lam_init):
    j = pl.program_id(1)
    n = q_ref.shape[0]

    @pl.when(j == 0)
    def _():
        _reset_rows(m_ref, l_ref, acc_ref)

    for h in range(H_C):
        q = q_ref[:, h * HEAD_DIM:(h + 1) * HEAD_DIM]
        s = jnp.concatenate(
            [jnp.dot(q[:, :DIFF_DIM], kt_ref[h, 0].astype(BF16), preferred_element_type=F32),
             jnp.dot(q[:, DIFF_DIM:], kt_ref[h, 1].astype(BF16), preferred_element_type=F32)], axis=0)
        rows = pl.ds(h, vc_ref.shape[0] // H_C, stride=H_C)
        _online_rows(s, vc_ref[rows, :].astype(BF16), m_ref, l_ref, acc_ref, h)

    @pl.when(j == pl.num_programs(1) - 1)
    def _():
        lam = _lambda(lq1_ref, lk1_ref, lq2_ref, lk2_ref, lam_init)
        for h in range(H_C):
            hs = slice(h * HEAD_DIM, (h + 1) * HEAD_DIM)
            s = _qk(_stack_maps(q_ref[:, hs]), kn_ref[:, hs].astype(BF16))
            vn = vn_ref[pl.ds(h, n, stride=H_C), :]
            _online_rows(s, vn.astype(BF16), m_ref, l_ref, acc_ref, h)
            o = acc_ref[h] / l_ref[h]
            o_ref[:, hs] = _subln(o[:n] - lam * o[n:], g_ref[...], lam_init)


def _diff_sample(q3, kn, vn, ckt, cv, lams, g_subln, *, layer, lam_init):
    b, n, _ = q3.shape
    past = cv.shape[2] // H_C
    kv = _pick(past, SAMPLE_KV)
    new = lambda: pl.BlockSpec((None, None, n, WC), lambda bi, j: (layer, bi, 0, 0))
    return pl.pallas_call(
        functools.partial(_diff_sample_kernel, lam_init=lam_init),
        grid=(b, past // kv),
        in_specs=[
            pl.BlockSpec((None, n, WC), lambda bi, j: (bi, 0, (WA + WB) // WC)),
            new(), pl.BlockSpec((None, None, n * H_C, HEAD_DIM), lambda bi, j: (layer, bi, 0, 0)),
            pl.BlockSpec((None, None, H_C, 2, DIFF_DIM, kv), lambda bi, j: (layer, bi, 0, 0, 0, j)),
            pl.BlockSpec((None, None, kv * H_C, HEAD_DIM), lambda bi, j: (layer, bi, j, 0)),
        ] + _lam_specs(),
        out_specs=pl.BlockSpec((None, n, WC), lambda bi, j: (bi, 0, 0)),
        out_shape=jax.ShapeDtypeStruct((b, n, WC), BF16),
        scratch_shapes=_sample_scratch(H_C, 2 * n),
        compiler_params=_params("parallel", "arbitrary"),
        name="diff_sample",
    )(q3, kn, vn, ckt, cv, *lams, g_subln)


def _merge_kernel(x_ref, oa_ref, ob_ref, oc_ref, w_ref, o_ref):
    o = jnp.concatenate([oa_ref[...], ob_ref[...], oc_ref[...]], axis=1)
    o_ref[...] = x_ref[...] + jnp.dot(o, w_ref[...], preferred_element_type=F32)


def _merge(x, oa, ob, oc, w, *, tm):
    n, d = x.shape
    mix = w.shape[0]
    return pl.pallas_call(
        _merge_kernel,
        grid=(n // tm,),
        in_specs=[
            pl.BlockSpec((tm, d), lambda i: (i, 0)),
            pl.BlockSpec((tm, oa.shape[1]), lambda i: (i, 0)),
            pl.BlockSpec((tm, ob.shape[1]), lambda i: (i, 0)),
            pl.BlockSpec((tm, oc.shape[1]), lambda i: (i, 0)),
            pl.BlockSpec((mix, d), lambda i: (0, 0)),
        ],
        out_specs=pl.BlockSpec((tm, d), lambda i: (i, 0)),
        out_shape=jax.ShapeDtypeStruct((n, d), F32),
        compiler_params=_params("parallel"),
        name="merge",
    )(x, oa, ob, oc, w)


def _mlp_kernel(x_ref, g_ref, wu_ref, wd_ref, o_ref, h_ref):
    @pl.when(pl.program_id(1) == 0)
    def _():
        x = x_ref[...]
        h_ref[...] = _rms_rows(x, g_ref[...]).astype(BF16)
        o_ref[...] = x

    u = jnp.dot(h_ref[...], wu_ref[...], preferred_element_type=F32)
    a = jnp.square(jnp.maximum(u, 0.0)).astype(BF16)
    o_ref[...] += jnp.dot(a, wd_ref[...], preferred_element_type=F32)


def _mlp(x, g, wu, wd, *, tm, tf):
    n, d = x.shape
    ff = wu.shape[1]
    return pl.pallas_call(
        _mlp_kernel,
        grid=(n // tm, ff // tf),
        in_specs=[
            pl.BlockSpec((tm, d), lambda i, f: (i, 0)),
            pl.BlockSpec((1, d), lambda i, f: (0, 0)),
            pl.BlockSpec((d, tf), lambda i, f: (0, f)),
            pl.BlockSpec((tf, d), lambda i, f: (f, 0)),
        ],
        out_specs=pl.BlockSpec((tm, d), lambda i, f: (i, 0)),
        out_shape=jax.ShapeDtypeStruct((n, d), F32),
        scratch_shapes=[pltpu.VMEM((tm, d), BF16)],
        compiler_params=_params("parallel", "arbitrary"),
        name="mlp",
    )(x, g, wu, wd)


def _split_cols(w):
    sizes = [WA] * 3 + [H_A] + [WB] * 3 + [WC] * 3
    out, c = [], 0
    for s in sizes:
        out.append(w[..., c:c + s])
        c += s
    return out


def _rope_tables(pos):
    half = DIFF_DIM // 2
    inv = ROPE_THETA ** (-jnp.arange(half, dtype=F32) * 2.0 / DIFF_DIM)
    ang = pos.astype(F32)[:, None] * inv[None, :]
    cos, sin = jnp.cos(ang), jnp.sin(ang)
    cos_t = jnp.tile(cos, (1, LANES // half))
    sin_t = jnp.tile(jnp.concatenate([-sin, sin], axis=1), (1, LANES // DIFF_DIM))
    return cos_t, sin_t


def _band_bias_ext(rel_table):
    far = WINDOW_B - REL_MAX_PAST + (CHUNK - 1)
    tab = rel_table.astype(F32) * LOG2E
    ext = jnp.concatenate(
        [jnp.broadcast_to(tab[:, -1:], (tab.shape[0], far)), tab[:, ::-1],
         jnp.broadcast_to(tab[:, :1], (tab.shape[0], BIAS_EXT - far - tab.shape[1]))], axis=1)
    return ext.reshape(tab.shape[0], 1, BIAS_EXT)


def _band_bias_reversed(ext):
    n_off = BAND_KEYS + CHUNK - 1
    top = CHUNK - 1 + LANES - 1 + BM_EXT - BM_SPAN
    rev = ext[:, :, :n_off][:, :, ::-1]
    lead = top - (n_off - 1)
    return jnp.pad(rev, ((0, 0), (0, 0), (lead, BM_EXT - lead - n_off)))


def _block_diag_ones(block):
    r = jnp.arange(MXU_DIM)
    return (r[:, None] // block == r[None, :] // block).astype(BF16)


def _pick(n, pref):
    return pref if n % pref == 0 else n


def kernel(x_prompt, x_sample, cache_a_k, cache_a_v, cache_a_logf, cache_b_k, cache_b_v, cache_c_k, cache_c_v, w_in, b_f, g_qa, g_ka, g_qb, g_kb, rel_bias, g_qc, g_kc, lam_q1, lam_k1, lam_q2, lam_k2, g_subln, w_out, g_mix, g_mlp, w_up, w_down):
    depth = w_in.shape[0]
    bp, t, d = x_prompt.shape
    bs, ns, _ = x_sample.shape
    past = cache_a_k.shape[2]
    b_rows = cache_b_k.shape[2]
    assert ns == CHUNK and b_rows == WINDOW_B and past % CHUNK == 0 and t % WINDOW_B == 0
    assert rel_bias.shape[-1] == REL_MAX_PAST + CHUNK
    n_p, n_s = bp * t, bs * ns

    tm_p = _pick(n_p, 512)
    tm_s = _pick(n_s, 512)
    tq = _pick(t, 512)

    cos_p, sin_p = _rope_tables(jnp.arange(t))
    cos_s, sin_s = _rope_tables(past + jnp.arange(ns))
    cos_s = jnp.tile(cos_s, (tm_s // ns, 1))
    sin_s = jnp.tile(sin_s, (tm_s // ns, 1))
    c_width = -(-(past + ns) // LANES) * LANES
    g128 = _block_diag_ones(HEAD_DIM)
    g64 = _block_diag_ones(DIFF_DIM)

    xp = x_prompt.reshape(n_p, d)
    xs = x_sample.reshape(n_s, d)
    zeros = lambda *shape: jnp.zeros(shape, F32)
    bufs_p = (zeros(depth, n_p, WA), zeros(depth, n_p, WA), zeros(depth, bp, WC, t),
              zeros(depth, n_p * H_C, HEAD_DIM), zeros(depth, n_p, H_A))
    bufs_s = (zeros(depth, n_s, WA), zeros(depth, n_s, WA), zeros(depth, n_s, WC),
              zeros(depth, n_s * H_C, HEAD_DIM), zeros(depth, n_s, H_A))
    band_p = (zeros(depth, bp, WINDOW_B * H_B, HEAD_DIM), zeros(depth, bp, WINDOW_B * H_B, HEAD_DIM))
    band_s = (zeros(depth, bs, b_rows * H_B, HEAD_DIM), zeros(depth, bs, b_rows * H_B, HEAD_DIM))
    ones = jnp.ones((HEAD_DIM,), F32)
    cache_c_kt = jnp.transpose(cache_c_k, (0, 1, 3, 4, 5, 2))
    cache_logf_rows = jnp.transpose(cache_a_logf.astype(F32), (0, 1, 3, 2))
    frames_by_head = lambda a: a.reshape(a.shape[0], a.shape[1], a.shape[2] * a.shape[3], HEAD_DIM)
    cak, cav, cbk, cbv, ccv = map(frames_by_head, (cache_a_k, cache_a_v, cache_b_k, cache_b_v, cache_c_v))

    for l in range(depth):
        lam_init = 0.8 - 0.6 * math.exp(-0.3 * l)
        qa, ka, va, fa, qb, kb, vb, qc, kc, vc = _split_cols(w_in[l])
        gqc = jnp.tile(g_qc[l], 2)
        gkc = jnp.tile(g_kc[l], 2)
        q_scale = HEAD_DIM ** -0.5 * LOG2E
        prm = dict(
            w=jnp.concatenate([qa, qb, qc, ka, va, kb, vb, kc, vc], axis=1).astype(BF16),
            wf=jnp.pad(fa, ((0, 0), (0, LANES - H_A))).astype(BF16),
            bf=jnp.pad(b_f[l], (0, LANES - H_A)).reshape(1, LANES),
            gain=jnp.concatenate(
                [jnp.tile(g_qa[l] * q_scale, H_A), jnp.tile(g_qb[l] * q_scale, H_B),
                 jnp.tile(gqc * (DIFF_DIM ** -0.5 * LOG2E), H_C),
                 jnp.tile(g_ka[l], H_A), jnp.tile(ones, H_A),
                 jnp.tile(g_kb[l], H_B), jnp.tile(ones, H_B),
                 jnp.tile(gkc, H_C), jnp.tile(ones, H_C)]).reshape(1, P_WIDTH),
            g128=g128, g64=g64)
        gmix = g_mix[l].reshape(1, d)
        gmlp = g_mlp[l].reshape(1, d)
        gsub = g_subln[l].reshape(1, HEAD_DIM)
        lams = [a[l].reshape(1, DIFF_DIM) for a in (lam_q1, lam_k1, lam_q2, lam_k2)]
        wo = w_out[l].astype(BF16)
        wu = w_up[l].astype(BF16)
        wd = w_down[l].astype(BF16)
        ext = _band_bias_ext(rel_bias[l])
        ext_rev = _band_bias_reversed(ext)

        qs, kbvb, bufs_p = _project(xp, gmix, prm, cos_p, sin_p, bufs_p, tm=tm_p, layer=l, kc_rows=t)
        ka_all, va_all, kct_all, vc_all, lf_all = bufs_p
        q3 = qs.reshape(bp, t, Q_WIDTH)
        logf_rows = jnp.transpose(lf_all[l].reshape(bp, t, H_A), (0, 2, 1)).reshape(bp * H_A, t)
        c_p = _cumsum_lanes(logf_rows)
        oa = _fox_prompt(q3, ka_all.reshape(depth, bp, t, WA), va_all.reshape(depth, bp, t, WA), c_p,
                         layer=l, tq=tq)
        ob, band_p = _band_prompt(q3, kbvb.reshape(bp, t, 2 * WB), ext_rev, band_p, layer=l)
        oc = _diff_prompt(q3, kct_all, vc_all.reshape(depth, bp, t * H_C, HEAD_DIM),
                          lams, gsub, layer=l, tq=tq, lam_init=lam_init)
        xp = _merge(xp, oa.reshape(n_p, -1), ob.reshape(n_p, -1), oc.reshape(n_p, -1), wo, tm=_pick(n_p, 512))
        xp = _mlp(xp, gmlp, wu, wd, tm=_pick(n_p, 512), tf=1024)

        qs, kbvb, bufs_s = _project(xs, gmix, prm, cos_s, sin_s, bufs_s, tm=tm_s, layer=l)
        ka_all, va_all, kc_all, vc_all, lf_all = bufs_s
        q3 = qs.reshape(bs, ns, Q_WIDTH)
        la_rows = jnp.transpose(lf_all[l].reshape(bs, ns, H_A), (0, 2, 1))
        logf_all = jnp.concatenate(
            [cache_logf_rows[l], la_rows, jnp.zeros((bs, H_A, c_width - past - ns), F32)], axis=2)
        c_s = _cumsum_lanes(logf_all.reshape(bs * H_A, c_width)).reshape(bs, H_A, c_width)
        oa = _fox_sample(q3, ka_all.reshape(depth, bs, ns, WA), va_all.reshape(depth, bs, ns, WA),
                         cak, cav, c_s, layer=l)
        ob, band_s = _band_sample(q3, kbvb.reshape(bs, ns, 2 * WB), cbk, cbv, ext, band_s,
                                  layer=l, first_key_pos=past - b_rows)
        oc = _diff_sample(q3, kc_all.reshape(depth, bs, ns, WC), vc_all.reshape(depth, bs, ns * H_C, HEAD_DIM),
                          cache_c_kt, ccv, lams, gsub, layer=l, lam_init=lam_init)
        xs = _merge(xs, oa.reshape(n_s, -1), ob.reshape(n_s, -1), oc.reshape(n_s, -1), wo, tm=_pick(n_s, 512))
        xs = _mlp(xs, gmlp, wu, wd, tm=_pick(n_s, 512), tf=1024)

    pak, pav, pckt, pcv, pal = bufs_p
    sak, sav, sck, scv, sal = bufs_s
    pck = jnp.transpose(pckt.reshape(depth, bp, H_C, 2, DIFF_DIM, t), (0, 1, 5, 2, 3, 4))
    return (xp.reshape(bp, t, d), xs.reshape(bs, ns, d),
            pak.reshape(depth, bp, t, H_A, HEAD_DIM), pav.reshape(depth, bp, t, H_A, HEAD_DIM),
            pal.reshape(depth, bp, t, H_A),
            band_p[0].reshape(depth, bp, WINDOW_B, H_B, HEAD_DIM), band_p[1].reshape(depth, bp, WINDOW_B, H_B, HEAD_DIM),
            pck, pcv.reshape(depth, bp, t, H_C, HEAD_DIM),
            sak.reshape(depth, bs, ns, H_A, HEAD_DIM), sav.reshape(depth, bs, ns, H_A, HEAD_DIM),
            sal.reshape(depth, bs, ns, H_A),
            band_s[0].reshape(depth, bs, b_rows, H_B, HEAD_DIM), band_s[1].reshape(depth, bs, b_rows, H_B, HEAD_DIM),
            sck.reshape(depth, bs, ns, H_C, 2, DIFF_DIM), scv.reshape(depth, bs, ns, H_C, HEAD_DIM))
```

```python
import functools
import math

import jax
import jax.numpy as jnp
from jax import lax
from jax.experimental import pallas as pl
from jax.experimental.pallas import tpu as pltpu

F32 = jnp.float32
BF16 = jnp.bfloat16

CHUNK = 64
HEAD_DIM = 128
H_A = 8
H_B = 4
H_C = 4
DIFF_DIM = HEAD_DIM // 2
BAND_CHUNKS = 8
WINDOW_B = BAND_CHUNKS * CHUNK
BAND_KEYS = WINDOW_B + CHUNK
REL_MAX_PAST = 128
ROPE_THETA = 10000.0
EPS = 1e-6
NEG = -1e30
LOG2E = math.log2(math.e)

LANES = 128
MXU_DIM = 256
VMEM_LIMIT = 52 * 1024 * 1024

WA, WB, WC = H_A * HEAD_DIM, H_B * HEAD_DIM, H_C * HEAD_DIM
Q_WIDTH = WA + WB + WC
QB_BLK = H_A
QC_BLK = H_A + H_B
P_WIDTH = 3 * (WA + WB + WC)
TN = 1536


def _params(*sem):
    return pltpu.CompilerParams(dimension_semantics=sem, vmem_limit_bytes=VMEM_LIMIT)


def _rms_rows(x, g):
    ms = jnp.mean(x * x, axis=-1, keepdims=True)
    return x * lax.rsqrt(ms + EPS) * g


def _log_sigmoid(z):
    return jnp.minimum(z, 0.0) - jnp.log1p(jnp.exp(-jnp.abs(z)))


ROW_CHUNK = 256


N_ALIAS = 5


def _proj_kernel(*refs, tm, kc_transposed):
    (x_ref, g_ref, w_ref, wf_ref, bf_ref, gain_ref, cos_ref, sin_ref, g128_ref, g64_ref) = refs[:10]
    (q_ref, kbvb_ref, ka_ref, va_ref, kc_ref, vc_ref, lf_ref, h_ref, acc_ref) = refs[10 + N_ALIAS:]
    j = pl.program_id(1)

    @pl.when(j == 0)
    def _():
        h = _rms_rows(x_ref[...], g_ref[...]).astype(BF16)
        h_ref[...] = h
        f = jnp.dot(h, wf_ref[...], preferred_element_type=F32) + bf_ref[...]
        lf_ref[...] = _log_sigmoid(f)[:, :H_A]

    def finish(tile, c0, c1, mode, dst_ref, d0, layout="rows"):
        rc = min(ROW_CHUNK, tm)
        for r0 in range(0, tm, rc):
            rs = slice(r0, r0 + rc)
            for c in range(c0, c1, MXU_DIM):
                cs = slice(c, c + MXU_DIM)
                x = acc_ref[rs, cs]
                if mode == "id":
                    y = x
                else:
                    gm, hd = (g128_ref, HEAD_DIM) if mode == "n128" else (g64_ref, DIFF_DIM)
                    ssq = jnp.dot((x * x).astype(BF16), gm[...], preferred_element_type=F32)
                    gain = gain_ref[:, tile * TN + c:tile * TN + c + MXU_DIM]
                    y = x * lax.rsqrt(ssq * (1.0 / hd) + EPS) * gain
                if mode == "n64r":
                    lane = lax.broadcasted_iota(jnp.int32, y.shape, 1)
                    first = (lane % DIFF_DIM) < (DIFF_DIM // 2)
                    cos = jnp.concatenate([cos_ref[rs, :]] * (MXU_DIM // LANES), axis=1)
                    sin = jnp.concatenate([sin_ref[rs, :]] * (MXU_DIM // LANES), axis=1)
                    partner = jnp.where(first, pltpu.roll(y, MXU_DIM - DIFF_DIM // 2, 1),
                                        pltpu.roll(y, DIFF_DIM // 2, 1))
                    y = y * cos + partner * sin
                ds = slice(d0 + c - c0, d0 + c - c0 + MXU_DIM)
                if layout == "transposed":
                    dst_ref[ds, rs] = y.T.astype(dst_ref.dtype)
                elif layout == "head_rows":
                    heads = (c1 - c0) // HEAD_DIM
                    for k in range(MXU_DIM // HEAD_DIM):
                        head = ds.start // HEAD_DIM + k
                        dst_ref[pl.ds(r0 * heads + head, rc, stride=heads), :] = \
                            y[:, k * HEAD_DIM:(k + 1) * HEAD_DIM]
                else:
                    dst_ref[rs, ds] = y.astype(dst_ref.dtype)

    plan = (
        ((0, WA + WB, "n128", q_ref, 0),),
        ((0, WC, "n64r", q_ref, WA + WB), (WC, WC + WA, "n128", ka_ref, 0)),
        ((0, WA, "id", va_ref, 0), (WA, WA + WB, "n128", kbvb_ref, 0)),
        ((0, WB, "id", kbvb_ref, WB),
         (WB, WB + WC, "n64r", kc_ref, 0, "transposed" if kc_transposed else "rows"),
         (WB + WC, WB + 2 * WC, "id", vc_ref, 0, "head_rows")),
    )
    acc_ref[...] = jnp.dot(h_ref[...], w_ref[...], preferred_element_type=F32)
    for jj, pieces in enumerate(plan):
        @pl.when(j == jj)
        def _(jj=jj, pieces=pieces):
            for piece in pieces:
                finish(jj, *piece)


def _project(x, g_mix, prm, cos_t, sin_t, bufs, *, tm, layer, kc_rows=None):
    n, d = x.shape
    period = cos_t.shape[0] // tm
    n_tiles = P_WIDTH // TN
    depth = bufs[0].shape[0]
    stack = lambda width: jax.ShapeDtypeStruct((depth, n, width), F32)
    lay = lambda width: pl.BlockSpec((None, tm, width), lambda i, j: (layer, i, 0))
    const = lambda shape: pl.BlockSpec(shape, lambda i, j: (0,) * len(shape))
    if kc_rows is None:
        kc_shape, kc_spec = stack(WC), lay(WC)
    else:
        per = kc_rows // tm
        kc_shape = jax.ShapeDtypeStruct((depth, n // kc_rows, WC, kc_rows), F32)
        kc_spec = pl.BlockSpec((None, None, WC, tm), lambda i, j: (layer, i // per, 0, i % per))
    in_specs = [
        pl.BlockSpec((tm, d), lambda i, j: (i, 0)),
        const((1, d)),
        pl.BlockSpec((d, TN), lambda i, j: (0, j)),
        const((d, LANES)),
        const((1, LANES)),
        const((1, P_WIDTH)),
        pl.BlockSpec((tm, LANES), lambda i, j: (i % period, 0)),
        pl.BlockSpec((tm, LANES), lambda i, j: (i % period, 0)),
        const((MXU_DIM, MXU_DIM)),
        const((MXU_DIM, MXU_DIM)),
    ] + [pl.BlockSpec(memory_space=pl.ANY)] * N_ALIAS
    assert len(bufs) == N_ALIAS
    out = pl.pallas_call(
        functools.partial(_proj_kernel, tm=tm, kc_transposed=kc_rows is not None),
        grid=(n // tm, n_tiles),
        in_specs=in_specs,
        out_specs=[
            pl.BlockSpec((tm, Q_WIDTH), lambda i, j: (i, 0)),
            pl.BlockSpec((tm, 2 * WB), lambda i, j: (i, 0)),
            lay(WA), lay(WA), kc_spec,
            pl.BlockSpec((None, tm * H_C, HEAD_DIM), lambda i, j: (layer, i, 0)), lay(H_A),
        ],
        out_shape=[
            jax.ShapeDtypeStruct((n, Q_WIDTH), BF16),
            jax.ShapeDtypeStruct((n, 2 * WB), F32),
            stack(WA), stack(WA), kc_shape,
            jax.ShapeDtypeStruct((depth, n * H_C, HEAD_DIM), F32), stack(H_A),
        ],
        input_output_aliases={10 + k: 2 + k for k in range(N_ALIAS)},
        scratch_shapes=[pltpu.VMEM((tm, d), BF16), pltpu.VMEM((tm, TN), F32)],
        compiler_params=_params("parallel", "arbitrary"),
        name="proj",
    )(x, g_mix, prm["w"], prm["wf"], prm["bf"], prm["gain"], cos_t, sin_t, prm["g128"], prm["g64"], *bufs)
    return out[0], out[1], tuple(out[2:])


def _cumsum_kernel(x_ref, o_ref):
    x = x_ref[...]
    n = x.shape[1]
    lane = lax.broadcasted_iota(jnp.int32, x.shape, 1)
    s = 1
    while s < n:
        x = x + jnp.where(lane >= s, pltpu.roll(x, s, 1), 0.0)
        s *= 2
    o_ref[...] = x


def _cumsum_lanes(x):
    r, n = x.shape
    rb = 8
    return pl.pallas_call(
        _cumsum_kernel,
        grid=(r // rb,),
        in_specs=[pl.BlockSpec((rb, n), lambda i: (i, 0))],
        out_specs=pl.BlockSpec((rb, n), lambda i: (i, 0)),
        out_shape=jax.ShapeDtypeStruct((r, n), F32),
        compiler_params=_params("parallel"),
        name="cumsum",
    )(x)


V_ROWS = HEAD_DIM + 16


def _flash_t(heads, mask_fn, *, tk, n_full):
    for _, _, _, m_ref, acc_ref, _ in heads:
        m_ref[...] = jnp.full(m_ref.shape, -jnp.inf, F32)
        acc_ref[...] = jnp.zeros(acc_ref.shape, F32)

    def scores(kb, slot):
        ks = pl.ds(pl.multiple_of(kb * tk, tk), tk)
        for q, kbf, _, _, _, s_ref in heads:
            s_ref[slot] = lax.dot_general(kbf[ks, :], q, (((1,), (1,)), ((), ())), preferred_element_type=F32)

    def softmax_pv(kb, slot, masked):
        for _, _, vt, m_ref, acc_ref, s_ref in heads:
            st = s_ref[slot]
            if masked:
                st = mask_fn(st)
            m_prev = m_ref[...]
            m_new = jnp.maximum(m_prev, jnp.max(st, axis=0, keepdims=True))
            m_ref[...] = m_new
            pt = jnp.exp2(st - m_new).astype(BF16)
            alpha = jnp.exp2(m_prev - m_new)
            acc_ref[...] = alpha * acc_ref[...] + jnp.dot(vt[kb], pt, preferred_element_type=F32)

    scores(0, 0)

    def pair(jp, carry):
        scores(2 * jp + 1, 1)
        softmax_pv(2 * jp, 0, False)
        scores(2 * jp + 2, 0)
        softmax_pv(2 * jp + 1, 1, False)
        return carry

    lax.fori_loop(0, n_full // 2, pair, 0)
    odd = n_full % 2 == 1

    @pl.when(odd)
    def _():
        scores(n_full, 1)
        softmax_pv(n_full - 1, 0, False)
        softmax_pv(n_full, 1, True)

    @pl.when(jnp.logical_not(odd))
    def _():
        softmax_pv(n_full, 0, True)


def _fill_vt(vt, v_ref, tk, head=0, heads=1):
    ones_row = jnp.where(lax.broadcasted_iota(jnp.int32, (V_ROWS - HEAD_DIM, tk), 0) == 0, 1.0, 0.0)
    for kb in range(vt.shape[0]):
        rows = pl.ds(kb * tk * heads + head, tk, stride=heads) if heads > 1 else slice(kb * tk, (kb + 1) * tk)
        vt[kb, :HEAD_DIM, :] = v_ref[rows, :].T.astype(BF16)
        vt[kb, HEAD_DIM:, :] = ones_row.astype(BF16)


def _split3(c):
    hi = c.astype(BF16).astype(F32)
    r1 = c - hi
    mid = r1.astype(BF16).astype(F32)
    lo = (r1 - mid).astype(BF16).astype(F32)
    return hi, mid, lo


def _decay_lanes(c_rep, key_side):
    hi, mid, lo = _split3(c_rep)
    lane = lax.broadcasted_iota(jnp.int32, c_rep.shape, 1)
    if key_side:
        parts = (-hi, -mid, -lo, 1.0, 1.0, 1.0)
    else:
        parts = (1.0, 1.0, 1.0, hi, mid, lo)
    vals = jnp.zeros(c_rep.shape, F32)
    for idx, part in enumerate(parts):
        vals = jnp.where(lane == idx, part, vals)
    return vals.astype(BF16)


HEAD_GROUP = 2


def _fox_kernel(q_ref, k_ref, v_ref, crow_ref, o_ref, kbf, vt, crep, m_ref, acc_ref, s_ref, *, tq):
    n_blocks = vt.shape[1]
    for g in range(HEAD_GROUP):
        gs = slice(g * HEAD_DIM, (g + 1) * HEAD_DIM)
        kbf[g, :, :HEAD_DIM] = k_ref[:, gs].astype(BF16)
        _fill_vt(vt.at[g], v_ref.at[:, gs], tq)
        for kb in range(n_blocks):
            rs = slice(kb * tq, (kb + 1) * tq)
            c_rep = jnp.broadcast_to(crow_ref[g, :, rs] * LOG2E, (LANES, tq)).T
            crep[g, rs, :] = c_rep
            kbf[g, rs, HEAD_DIM:] = _decay_lanes(c_rep, True)

    def mask_fn(st):
        kpos = lax.broadcasted_iota(jnp.int32, st.shape, 0)
        qpos = lax.broadcasted_iota(jnp.int32, st.shape, 1)
        return jnp.where(kpos <= qpos, st, NEG)

    def q_block(i, carry):
        rows = pl.ds(pl.multiple_of(i * tq, tq), tq)
        heads = []
        for g in range(HEAD_GROUP):
            gs = slice(g * HEAD_DIM, (g + 1) * HEAD_DIM)
            q = jnp.concatenate([q_ref[rows, gs], _decay_lanes(crep[g, rows, :], False)], axis=1)
            heads.append((q, kbf.at[g], vt.at[g], m_ref.at[g], acc_ref.at[g], s_ref.at[g]))
        _flash_t(heads, mask_fn, tk=tq, n_full=i)
        for g in range(HEAD_GROUP):
            acc = acc_ref[g]
            o_ref[rows, g * HEAD_DIM:(g + 1) * HEAD_DIM] = \
                (acc[:HEAD_DIM] / acc[HEAD_DIM:HEAD_DIM + 1]).T.astype(BF16)
        return carry

    lax.fori_loop(0, n_blocks, q_block, 0)


def _fox_prompt(q3, ka, va, c, *, layer, tq):
    b, t, _ = q3.shape
    crow = c.reshape(b, H_A, 1, t)
    gw = HEAD_GROUP * HEAD_DIM
    kv = lambda: pl.BlockSpec((None, None, t, gw), lambda bi, h: (layer, bi, 0, h))
    return pl.pallas_call(
        functools.partial(_fox_kernel, tq=tq),
        grid=(b, H_A // HEAD_GROUP),
        in_specs=[
            pl.BlockSpec((None, t, gw), lambda bi, h: (bi, 0, h)),
            kv(), kv(),
            pl.BlockSpec((None, HEAD_GROUP, 1, t), lambda bi, h: (bi, h, 0, 0)),
        ],
        out_specs=pl.BlockSpec((None, t, gw), lambda bi, h: (bi, 0, h)),
        out_shape=jax.ShapeDtypeStruct((b, t, WA), BF16),
        scratch_shapes=[
            pltpu.VMEM((HEAD_GROUP, t, 2 * HEAD_DIM), BF16),
            pltpu.VMEM((HEAD_GROUP, t // tq, V_ROWS, tq), BF16),
            pltpu.VMEM((HEAD_GROUP, t, LANES), F32),
            pltpu.VMEM((HEAD_GROUP, 1, tq), F32), pltpu.VMEM((HEAD_GROUP, V_ROWS, tq), F32),
            pltpu.VMEM((HEAD_GROUP, 2, tq, tq), F32),
        ],
        compiler_params=_params("parallel", "parallel"),
        name="fox_prompt",
    )(q3, ka, va, crow)


def _lambda(lq1_ref, lk1_ref, lq2_ref, lk2_ref, lam_init):
    a = jnp.sum(lq1_ref[...] * lk1_ref[...], axis=-1, keepdims=True)
    b = jnp.sum(lq2_ref[...] * lk2_ref[...], axis=-1, keepdims=True)
    return jnp.exp(a) - jnp.exp(b) + lam_init


def _stack_maps(q):
    lane = lax.broadcasted_iota(jnp.int32, q.shape, 1)
    lo = lane < DIFF_DIM
    zero = jnp.zeros_like(q)
    return jnp.concatenate([jnp.where(lo, q, zero), jnp.where(lo, zero, q)], axis=0)


def _subln(o, g, lam_init):
    return (_rms_rows(o, g) * (1.0 - lam_init)).astype(BF16)


def _diff_kernel(q_ref, kt_ref, v_ref, lq1_ref, lk1_ref, lq2_ref, lk2_ref, g_ref, o_ref,
                 kbf, vt, m_ref, acc_ref, s_ref, *, tq, lam_init):
    n_blocks = vt.shape[1]
    for g in range(HEAD_GROUP):
        gs = slice(g * HEAD_DIM, (g + 1) * HEAD_DIM)
        for kb in range(n_blocks):
            rs = slice(kb * tq, (kb + 1) * tq)
            kbf[g, rs, :] = kt_ref[gs, rs].T.astype(BF16)
        for hh in range(H_C // HEAD_GROUP):
            @pl.when(pl.program_id(1) == hh)
            def _(g=g, head=hh * HEAD_GROUP + g):
                _fill_vt(vt.at[g], v_ref, tq, head=head, heads=H_C)
    lam = _lambda(lq1_ref, lk1_ref, lq2_ref, lk2_ref, lam_init)

    def mask_fn(st):
        kpos = lax.broadcasted_iota(jnp.int32, st.shape, 0)
        qpos = lax.broadcasted_iota(jnp.int32, st.shape, 1) % tq
        return jnp.where(kpos // CHUNK <= qpos // CHUNK, st, NEG)

    def q_block(i, carry):
        rows = pl.ds(pl.multiple_of(i * tq, tq), tq)
        heads = [(_stack_maps(q_ref[rows, g * HEAD_DIM:(g + 1) * HEAD_DIM]),
                  kbf.at[g], vt.at[g], m_ref.at[g], acc_ref.at[g], s_ref.at[g]) for g in range(HEAD_GROUP)]
        _flash_t(heads, mask_fn, tk=tq, n_full=i)
        for g in range(HEAD_GROUP):
            acc = acc_ref[g]
            o = acc[:HEAD_DIM] / acc[HEAD_DIM:HEAD_DIM + 1]
            o_ref[rows, g * HEAD_DIM:(g + 1) * HEAD_DIM] = \
                _subln((o[:, :tq] - lam * o[:, tq:]).T, g_ref[...], lam_init)
        return carry

    lax.fori_loop(0, n_blocks, q_block, 0)


def _lam_specs():
    return [pl.BlockSpec((1, DIFF_DIM), lambda *a: (0, 0)) for _ in range(4)] + \
           [pl.BlockSpec((1, LANES), lambda *a: (0, 0))]


def _diff_prompt(q3, kct, vc, lams, g_subln, *, layer, tq, lam_init):
    b, t, _ = q3.shape
    gw = HEAD_GROUP * HEAD_DIM
    return pl.pallas_call(
        functools.partial(_diff_kernel, tq=tq, lam_init=lam_init),
        grid=(b, H_C // HEAD_GROUP),
        in_specs=[
            pl.BlockSpec((None, t, gw), lambda bi, h: (bi, 0, QC_BLK // HEAD_GROUP + h)),
            pl.BlockSpec((None, None, gw, t), lambda bi, h: (layer, bi, h, 0)),
            pl.BlockSpec((None, None, t * H_C, HEAD_DIM), lambda bi, h: (layer, bi, 0, 0)),
        ] + _lam_specs(),
        out_specs=pl.BlockSpec((None, t, gw), lambda bi, h: (bi, 0, h)),
        out_shape=jax.ShapeDtypeStruct((b, t, WC), BF16),
        scratch_shapes=[
            pltpu.VMEM((HEAD_GROUP, t, HEAD_DIM), BF16), pltpu.VMEM((HEAD_GROUP, t // tq, V_ROWS, tq), BF16),
            pltpu.VMEM((HEAD_GROUP, 1, 2 * tq), F32), pltpu.VMEM((HEAD_GROUP, V_ROWS, 2 * tq), F32),
            pltpu.VMEM((HEAD_GROUP, 2, tq, 2 * tq), F32),
        ],
        compiler_params=_params("parallel", "parallel"),
        name="diff_prompt",
    )(q3, kct, vc, *lams, g_subln)


BIAS_EXT = 5 * LANES


def _band_bias_tile(ext):
    x = jnp.broadcast_to(ext, (CHUNK, BIAS_EXT))
    return pltpu.roll(x, BIAS_EXT - (CHUNK - 1), 1, stride=1, stride_axis=0)[:, :BAND_KEYS]


def _band_chunk(q, k, v, bias, first_key_pos):
    s = lax.dot_general(q, k, (((1,), (1,)), ((), ())), preferred_element_type=F32) + bias
    kpos = first_key_pos + lax.broadcasted_iota(jnp.int32, s.shape, 1)
    s = jnp.where(kpos >= 0, s, NEG)
    m = jnp.max(s, axis=1, keepdims=True)
    p = jnp.exp2(s - m)
    l = jnp.sum(p, axis=1, keepdims=True)
    return (jnp.dot(p.astype(BF16), v, preferred_element_type=F32) / l).astype(BF16)


BAND_STEP = 256
BAND_WIN = BAND_STEP + WINDOW_B
BM_SPAN = BAND_STEP + LANES
BM_EXT = BM_SPAN + (BAND_WIN // LANES - 1) * LANES


def _band_biasmask(g_ref, bm_ref):
    n_kb = BAND_WIN // LANES
    for kb in range(n_kb):
        start = LANES * (n_kb - 1 - kb)
        x = jnp.broadcast_to(g_ref[:, start:start + BM_SPAN], (LANES, BM_SPAN))
        tile = pltpu.roll(x, BM_SPAN - (LANES - 1), 1, stride=1, stride_axis=0)[:, :BAND_STEP]
        k = kb * LANES + lax.broadcasted_iota(jnp.int32, (LANES, BAND_STEP), 0)
        first = (lax.broadcasted_iota(jnp.int32, (LANES, BAND_STEP), 1) // CHUNK) * CHUNK
        tile = jnp.where(k >= first, tile, NEG)
        bm_ref[kb * LANES:(kb + 1) * LANES, :] = jnp.where(k < first + BAND_KEYS, tile, NEG)


N_BAND_ALIAS = 2


def _band_kernel(*refs, t):
    q_ref, k_ref, v_ref, g_ref = refs[:4]
    o_ref, pbk_ref, pbv_ref, kpad, vt, bm_ref = refs[4 + N_BAND_ALIAS:]
    hg = pl.program_id(1)
    n_steps = t // BAND_STEP
    pad_chunks = WINDOW_B // BAND_STEP
    ones_row = jnp.where(lax.broadcasted_iota(jnp.int32, (V_ROWS - HEAD_DIM, BAND_STEP), 0) == 0, 1.0, 0.0)
    for g in range(HEAD_GROUP):
        gs = slice(g * HEAD_DIM, (g + 1) * HEAD_DIM)
        kpad[g, :WINDOW_B, :] = jnp.zeros((WINDOW_B, LANES), BF16)
        kpad[g, WINDOW_B:, :] = k_ref[:, gs].astype(BF16)
        for c in range(pad_chunks):
            vt[g, c] = jnp.zeros((V_ROWS, BAND_STEP), BF16)
        for kb in range(n_steps):
            vt[g, pad_chunks + kb, :HEAD_DIM, :] = v_ref[kb * BAND_STEP:(kb + 1) * BAND_STEP, gs].T.astype(BF16)
            vt[g, pad_chunks + kb, HEAD_DIM:, :] = ones_row.astype(BF16)
        _band_biasmask(g_ref.at[g], bm_ref.at[g])
        for hh in range(H_B // HEAD_GROUP):
            @pl.when(hg == hh)
            def _(head=hh * HEAD_GROUP + g, gs=gs):
                pbk_ref[pl.ds(head, WINDOW_B, stride=H_B), :] = k_ref[t - WINDOW_B:, gs]
                pbv_ref[pl.ds(head, WINDOW_B, stride=H_B), :] = v_ref[t - WINDOW_B:, gs]

    def step(i, carry):
        start = pl.multiple_of(i * BAND_STEP, BAND_STEP)
        rows = pl.ds(start, BAND_STEP)
        kpos = start - WINDOW_B + lax.broadcasted_iota(jnp.int32, (BAND_WIN, BAND_STEP), 0)
        for g in range(HEAD_GROUP):
            gs = slice(g * HEAD_DIM, (g + 1) * HEAD_DIM)
            st = lax.dot_general(kpad[g, pl.ds(start, BAND_WIN), :], q_ref[rows, gs], (((1,), (1,)), ((), ())),
                                 preferred_element_type=F32) + bm_ref[g]
            st = jnp.where(kpos >= 0, st, NEG)
            m = jnp.max(st, axis=0, keepdims=True)
            p = jnp.exp2(st - m).astype(BF16)
            acc = jnp.zeros((V_ROWS, BAND_STEP), F32)
            for c in range(BAND_WIN // BAND_STEP):
                acc += jnp.dot(vt[g, i + c], p[c * BAND_STEP:(c + 1) * BAND_STEP], preferred_element_type=F32)
            o_ref[rows, gs] = (acc[:HEAD_DIM] / acc[HEAD_DIM:HEAD_DIM + 1]).T.astype(BF16)
        return carry

    lax.fori_loop(0, n_steps, step, 0, unroll=2)


def _band_prompt(q3, kbvb, g, bufs, *, layer):
    b, t, _ = q3.shape
    assert len(bufs) == N_BAND_ALIAS
    gw = HEAD_GROUP * HEAD_DIM
    groups = H_B // HEAD_GROUP
    keep = jax.ShapeDtypeStruct(bufs[0].shape, F32)
    keep_spec = pl.BlockSpec((None, None, WINDOW_B * H_B, HEAD_DIM), lambda bi, hg: (layer, bi, 0, 0))
    out = pl.pallas_call(
        functools.partial(_band_kernel, t=t),
        grid=(b, groups),
        in_specs=[
            pl.BlockSpec((None, t, gw), lambda bi, hg: (bi, 0, QB_BLK // HEAD_GROUP + hg)),
            pl.BlockSpec((None, t, gw), lambda bi, hg: (bi, 0, hg)),
            pl.BlockSpec((None, t, gw), lambda bi, hg: (bi, 0, groups + hg)),
            pl.BlockSpec((HEAD_GROUP, 1, BM_EXT), lambda bi, hg: (hg, 0, 0)),
        ] + [pl.BlockSpec(memory_space=pl.ANY)] * N_BAND_ALIAS,
        out_specs=[pl.BlockSpec((None, t, gw), lambda bi, hg: (bi, 0, hg)), keep_spec, keep_spec],
        out_shape=[jax.ShapeDtypeStruct((b, t, WB), BF16), keep, keep],
        input_output_aliases={4 + k: 1 + k for k in range(N_BAND_ALIAS)},
        scratch_shapes=[pltpu.VMEM((HEAD_GROUP, t + WINDOW_B, LANES), BF16),
                        pltpu.VMEM((HEAD_GROUP, (t + WINDOW_B) // BAND_STEP, V_ROWS, BAND_STEP), BF16),
                        pltpu.VMEM((HEAD_GROUP, BAND_WIN, BAND_STEP), F32)],
        compiler_params=_params("parallel", "arbitrary"),
        name="band_prompt",
    )(q3, kbvb, kbvb, g, *bufs)
    return out[0], tuple(out[1:])


SAMPLE_KV = 2048


def _qk(q, k):
    return lax.dot_general(q, k, (((1,), (1,)), ((), ())), preferred_element_type=F32)


def _online_rows(s, v, m_ref, l_ref, acc_ref, h):
    m_prev = m_ref[h]
    m_new = jnp.maximum(m_prev, jnp.max(s, axis=1, keepdims=True))
    alpha = jnp.exp2(m_prev - m_new)
    p = jnp.exp2(s - m_new)
    l_ref[h] = alpha * l_ref[h] + jnp.sum(p, axis=1, keepdims=True)
    acc_ref[h] = alpha * acc_ref[h] + jnp.dot(p.astype(BF16), v, preferred_element_type=F32)
    m_ref[h] = m_new


def _reset_rows(m_ref, l_ref, acc_ref):
    m_ref[...] = jnp.full(m_ref.shape, -jnp.inf, F32)
    l_ref[...] = jnp.zeros(l_ref.shape, F32)
    acc_ref[...] = jnp.zeros(acc_ref.shape, F32)


def _fox_sample_kernel(q_ref, kn_ref, vn_ref, kc_ref, vc_ref, crow_ref, cnew_ref, ccol_ref, o_ref,
                       m_ref, l_ref, acc_ref):
    j = pl.program_id(1)
    n = q_ref.shape[0]

    @pl.when(j == 0)
    def _():
        _reset_rows(m_ref, l_ref, acc_ref)

    for h in range(H_A):
        hs = slice(h * HEAD_DIM, (h + 1) * HEAD_DIM)
        rows = pl.ds(h, kc_ref.shape[0] // H_A, stride=H_A)
        s = _qk(q_ref[:, hs], kc_ref[rows, :].astype(BF16)) + \
            (ccol_ref[h] - crow_ref[h:h + 1, :]) * LOG2E
        _online_rows(s, vc_ref[rows, :].astype(BF16), m_ref, l_ref, acc_ref, h)

    @pl.when(j == pl.num_programs(1) - 1)
    def _():
        row = lax.broadcasted_iota(jnp.int32, (n, n), 0)
        col = lax.broadcasted_iota(jnp.int32, (n, n), 1)
        for h in range(H_A):
            hs = slice(h * HEAD_DIM, (h + 1) * HEAD_DIM)
            s = _qk(q_ref[:, hs], kn_ref[:, hs].astype(BF16)) + \
                (ccol_ref[h] - cnew_ref[h:h + 1, :n]) * LOG2E
            _online_rows(jnp.where(col <= row, s, NEG), vn_ref[:, hs].astype(BF16), m_ref, l_ref, acc_ref, h)
            o_ref[:, hs] = (acc_ref[h] / l_ref[h]).astype(BF16)


def _sample_scratch(heads, rows):
    return [pltpu.VMEM((heads, rows, 1), F32), pltpu.VMEM((heads, rows, 1), F32),
            pltpu.VMEM((heads, rows, HEAD_DIM), F32)]


def _fox_sample(q3, kn, vn, ck, cv, c, *, layer):
    b, n, _ = q3.shape
    past = ck.shape[2] // H_A
    kv = _pick(past, SAMPLE_KV)
    new = lambda: pl.BlockSpec((None, None, n, WA), lambda bi, j: (layer, bi, 0, 0))
    cache = lambda: pl.BlockSpec((None, None, kv * H_A, HEAD_DIM), lambda bi, j: (layer, bi, j, 0))
    ccol = c[:, :, past:past + n].reshape(b, H_A, n, 1)
    return pl.pallas_call(
        _fox_sample_kernel,
        grid=(b, past // kv),
        in_specs=[
            pl.BlockSpec((None, n, WA), lambda bi, j: (bi, 0, 0)),
            new(), new(), cache(), cache(),
            pl.BlockSpec((None, H_A, kv), lambda bi, j: (bi, 0, j)),
            pl.BlockSpec((None, H_A, LANES), lambda bi, j: (bi, 0, past // LANES)),
            pl.BlockSpec((None, H_A, n, 1), lambda bi, j: (bi, 0, 0, 0)),
        ],
        out_specs=pl.BlockSpec((None, n, WA), lambda bi, j: (bi, 0, 0)),
        out_shape=jax.ShapeDtypeStruct((b, n, WA), BF16),
        scratch_shapes=_sample_scratch(H_A, n),
        compiler_params=_params("parallel", "arbitrary"),
        name="fox_sample",
    )(q3, kn, vn, ck, cv, c, c, ccol)


def _band_sample_kernel(*refs, first_key_pos):
    q_ref, kbvb_ref, kc_ref, vc_ref, ext_ref = refs[:5]
    o_ref, sbk_ref, sbv_ref = refs[5 + N_BAND_ALIAS:]
    n = q_ref.shape[0]
    rows = kc_ref.shape[0] // H_B
    keep = rows - n
    sbk_ref[:keep * H_B, :] = kc_ref[n * H_B:, :]
    sbv_ref[:keep * H_B, :] = vc_ref[n * H_B:, :]
    for h in range(H_B):
        hs = slice(h * HEAD_DIM, (h + 1) * HEAD_DIM)
        kn = kbvb_ref[:, hs]
        vn = kbvb_ref[:, WB + h * HEAD_DIM:WB + (h + 1) * HEAD_DIM]
        new_rows = pl.ds(keep * H_B + h, n, stride=H_B)
        sbk_ref[new_rows, :] = kn
        sbv_ref[new_rows, :] = vn
        cached = pl.ds(h, rows, stride=H_B)
        k = jnp.concatenate([kc_ref[cached, :], kn], axis=0).astype(BF16)
        v = jnp.concatenate([vc_ref[cached, :], vn], axis=0).astype(BF16)
        o_ref[:, hs] = _band_chunk(q_ref[:, hs], k, v, _band_bias_tile(ext_ref[h]), first_key_pos)


def _band_sample(q3, kbvb, ck, cv, ext, bufs, *, layer, first_key_pos):
    b, n, _ = q3.shape
    rows_h = ck.shape[2]
    assert len(bufs) == N_BAND_ALIAS
    roll_spec = lambda: pl.BlockSpec((None, None, rows_h, HEAD_DIM), lambda bi: (layer, bi, 0, 0))
    rolled = jax.ShapeDtypeStruct(ck.shape, F32)
    out = pl.pallas_call(
        functools.partial(_band_sample_kernel, first_key_pos=first_key_pos),
        grid=(b,),
        in_specs=[
            pl.BlockSpec((None, n, WB), lambda bi: (bi, 0, WA // WB)),
            pl.BlockSpec((None, n, 2 * WB), lambda bi: (bi, 0, 0)),
            roll_spec(), roll_spec(),
            pl.BlockSpec((H_B, 1, BIAS_EXT), lambda bi: (0, 0, 0)),
        ] + [pl.BlockSpec(memory_space=pl.ANY)] * N_BAND_ALIAS,
        out_specs=[pl.BlockSpec((None, n, WB), lambda bi: (bi, 0, 0)), roll_spec(), roll_spec()],
        out_shape=[jax.ShapeDtypeStruct((b, n, WB), BF16), rolled, rolled],
        input_output_aliases={5 + k: 1 + k for k in range(N_BAND_ALIAS)},
        compiler_params=_params("parallel"),
        name="band_sample",
    )(q3, kbvb, ck, cv, ext, *bufs)
    return out[0], tuple(out[1:])


def _diff_sample_kernel(q_ref, kn_ref, vn_ref, kt_ref, vc_ref, lq1_ref, lk1_ref, lq2_ref, lk2_ref,
                        g_ref, o_ref, m_ref, l_ref, acc_ref, *, lam_init):
    j = pl.program_id(1)
    n = q_ref.shape[0]

    @pl.when(j == 0)
    def _():
        _reset_rows(m_ref, l_ref, acc_ref)

    for h in range(H_C):
        q = q_ref[:, h * HEAD_DIM:(h + 1) * HEAD_DIM]
        s = jnp.concatenate(
            [jnp.dot(q[:, :DIFF_DIM], kt_ref[h, 0].astype(BF16), preferred_element_type=F32),
             jnp.dot(q[:, DIFF_DIM:], kt_ref[h, 1].astype(BF16), preferred_element_type=F32)], axis=0)
        rows = pl.ds(h, vc_ref.shape[0] // H_C, stride=H_C)
        _online_rows(s, vc_ref[rows, :].astype(BF16), m_ref, l_ref, acc_ref, h)

    @pl.when(j == pl.num_programs(1) - 1)
    def _():
        lam = _lambda(lq1_ref, lk1_ref, lq2_ref, lk2_ref, lam_init)
        for h in range(H_C):
            hs = slice(h * HEAD_DIM, (h + 1) * HEAD_DIM)
            s = _qk(_stack_maps(q_ref[:, hs]), kn_ref[:, hs].astype(BF16))
            vn = vn_ref[pl.ds(h, n, stride=H_C), :]
            _online_rows(s, vn.astype(BF16), m_ref, l_ref, acc_ref, h)
            o = acc_ref[h] / l_ref[h]
            o_ref[:, hs] = _subln(o[:n] - lam * o[n:], g_ref[...], lam_init)


def _diff_sample(q3, kn, vn, ckt, cv, lams, g_subln, *, layer, lam_init):
    b, n, _ = q3.shape
    past = cv.shape[2] // H_C
    kv = _pick(past, SAMPLE_KV)
    new = lambda: pl.BlockSpec((None, None, n, WC), lambda bi, j: (layer, bi, 0, 0))
    return pl.pallas_call(
        functools.partial(_diff_sample_kernel, lam_init=lam_init),
        grid=(b, past // kv),
        in_specs=[
            pl.BlockSpec((None, n, WC), lambda bi, j: (bi, 0, (WA + WB) // WC)),
            new(), pl.BlockSpec((None, None, n * H_C, HEAD_DIM), lambda bi, j: (layer, bi, 0, 0)),
            pl.BlockSpec((None, None, H_C, 2, DIFF_DIM, kv), lambda bi, j: (layer, bi, 0, 0, 0, j)),
            pl.BlockSpec((None, None, kv * H_C, HEAD_DIM), lambda bi, j: (layer, bi, j, 0)),
        ] + _lam_specs(),
        out_specs=pl.BlockSpec((None, n, WC), lambda bi, j: (bi, 0, 0)),
        out_shape=jax.ShapeDtypeStruct((b, n, WC), BF16),
        scratch_shapes=_sample_scratch(H_C, 2 * n),
        compiler_params=_params("parallel", "arbitrary"),
        name="diff_sample",
    )(q3, kn, vn, ckt, cv, *lams, g_subln)


def _merge_kernel(x_ref, oa_ref, ob_ref, oc_ref, w_ref, o_ref):
    o = jnp.concatenate([oa_ref[...], ob_ref[...], oc_ref[...]], axis=1)
    o_ref[...] = x_ref[...] + jnp.dot(o, w_ref[...], preferred_element_type=F32)


def _merge(x, oa, ob, oc, w, *, tm):
    n, d = x.shape
    mix = w.shape[0]
    return pl.pallas_call(
        _merge_kernel,
        grid=(n // tm,),
        in_specs=[
            pl.BlockSpec((tm, d), lambda i: (i, 0)),
            pl.BlockSpec((tm, oa.shape[1]), lambda i: (i, 0)),
            pl.BlockSpec((tm, ob.shape[1]), lambda i: (i, 0)),
            pl.BlockSpec((tm, oc.shape[1]), lambda i: (i, 0)),
            pl.BlockSpec((mix, d), lambda i: (0, 0)),
        ],
        out_specs=pl.BlockSpec((tm, d), lambda i: (i, 0)),
        out_shape=jax.ShapeDtypeStruct((n, d), F32),
        compiler_params=_params("parallel"),
        name="merge",
    )(x, oa, ob, oc, w)


def _mlp_kernel(x_ref, g_ref, wu_ref, wd_ref, o_ref, h_ref):
    @pl.when(pl.program_id(1) == 0)
    def _():
        x = x_ref[...]
        h_ref[...] = _rms_rows(x, g_ref[...]).astype(BF16)
        o_ref[...] = x

    u = jnp.dot(h_ref[...], wu_ref[...], preferred_element_type=F32)
    a = jnp.square(jnp.maximum(u, 0.0)).astype(BF16)
    o_ref[...] += jnp.dot(a, wd_ref[...], preferred_element_type=F32)


def _mlp(x, g, wu, wd, *, tm, tf):
    n, d = x.shape
    ff = wu.shape[1]
    return pl.pallas_call(
        _mlp_kernel,
        grid=(n // tm, ff // tf),
        in_specs=[
            pl.BlockSpec((tm, d), lambda i, f: (i, 0)),
            pl.BlockSpec((1, d), lambda i, f: (0, 0)),
            pl.BlockSpec((d, tf), lambda i, f: (0, f)),
            pl.BlockSpec((tf, d), lambda i, f: (f, 0)),
        ],
        out_specs=pl.BlockSpec((tm, d), lambda i, f: (i, 0)),
        out_shape=jax.ShapeDtypeStruct((n, d), F32),
        scratch_shapes=[pltpu.VMEM((tm, d), BF16)],
        compiler_params=_params("parallel", "arbitrary"),
        name="mlp",
    )(x, g, wu, wd)


def _split_cols(w):
    sizes = [WA] * 3 + [H_A] + [WB] * 3 + [WC] * 3
    out, c = [], 0
    for s in sizes:
        out.append(w[..., c:c + s])
        c += s
    return out


def _rope_tables(pos):
    half = DIFF_DIM // 2
    inv = ROPE_THETA ** (-jnp.arange(half, dtype=F32) * 2.0 / DIFF_DIM)
    ang = pos.astype(F32)[:, None] * inv[None, :]
    cos, sin = jnp.cos(ang), jnp.sin(ang)
    cos_t = jnp.tile(cos, (1, LANES // half))
    sin_t = jnp.tile(jnp.concatenate([-sin, sin], axis=1), (1, LANES // DIFF_DIM))
    return cos_t, sin_t


def _band_bias_ext(rel_table):
    far = WINDOW_B - REL_MAX_PAST + (CHUNK - 1)
    tab = rel_table.astype(F32) * LOG2E
    ext = jnp.concatenate(
        [jnp.broadcast_to(tab[:, -1:], (tab.shape[0], far)), tab[:, ::-1],
         jnp.broadcast_to(tab[:, :1], (tab.shape[0], BIAS_EXT - far - tab.shape[1]))], axis=1)
    return ext.reshape(tab.shape[0], 1, BIAS_EXT)


def _band_bias_reversed(ext):
    n_off = BAND_KEYS + CHUNK - 1
    top = CHUNK - 1 + LANES - 1 + BM_EXT - BM_SPAN
    rev = ext[:, :, :n_off][:, :, ::-1]
    lead = top - (n_off - 1)
    return jnp.pad(rev, ((0, 0), (0, 0), (lead, BM_EXT - lead - n_off)))


def _block_diag_ones(block):
    r = jnp.arange(MXU_DIM)
    return (r[:, None] // block == r[None, :] // block).astype(BF16)


def _pick(n, pref):
    return pref if n % pref == 0 else n


def kernel(x_prompt, x_sample, cache_a_k, cache_a_v, cache_a_logf, cache_b_k, cache_b_v, cache_c_k, cache_c_v, w_in, b_f, g_qa, g_ka, g_qb, g_kb, rel_bias, g_qc, g_kc, lam_q1, lam_k1, lam_q2, lam_k2, g_subln, w_out, g_mix, g_mlp, w_up, w_down):
    depth = w_in.shape[0]
    bp, t, d = x_prompt.shape
    bs, ns, _ = x_sample.shape
    past = cache_a_k.shape[2]
    b_rows = cache_b_k.shape[2]
    assert ns == CHUNK and b_rows == WINDOW_B and past % CHUNK == 0 and t % WINDOW_B == 0
    assert rel_bias.shape[-1] == REL_MAX_PAST + CHUNK
    n_p, n_s = bp * t, bs * ns

    tm_p = _pick(n_p, 512)
    tm_s = _pick(n_s, 512)
    tq = _pick(t, 512)

    cos_p, sin_p = _rope_tables(jnp.arange(t))
    cos_s, sin_s = _rope_tables(past + jnp.arange(ns))
    cos_s = jnp.tile(cos_s, (tm_s // ns, 1))
    sin_s = jnp.tile(sin_s, (tm_s // ns, 1))
    c_width = -(-(past + ns) // LANES) * LANES
    g128 = _block_diag_ones(HEAD_DIM)
    g64 = _block_diag_ones(DIFF_DIM)

    xp = x_prompt.reshape(n_p, d)
    xs = x_sample.reshape(n_s, d)
    zeros = lambda *shape: jnp.zeros(shape, F32)
    bufs_p = (zeros(depth, n_p, WA), zeros(depth, n_p, WA), zeros(depth, bp, WC, t),
              zeros(depth, n_p * H_C, HEAD_DIM), zeros(depth, n_p, H_A))
    bufs_s = (zeros(depth, n_s, WA), zeros(depth, n_s, WA), zeros(depth, n_s, WC),
              zeros(depth, n_s * H_C, HEAD_DIM), zeros(depth, n_s, H_A))
    band_p = (zeros(depth, bp, WINDOW_B * H_B, HEAD_DIM), zeros(depth, bp, WINDOW_B * H_B, HEAD_DIM))
    band_s = (zeros(depth, bs, b_rows * H_B, HEAD_DIM), zeros(depth, bs, b_rows * H_B, HEAD_DIM))
    ones = jnp.ones((HEAD_DIM,), F32)
    cache_c_kt = jnp.transpose(cache_c_k, (0, 1, 3, 4, 5, 2))
    cache_logf_rows = jnp.transpose(cache_a_logf.astype(F32), (0, 1, 3, 2))
    frames_by_head = lambda a: a.reshape(a.shape[0], a.shape[1], a.shape[2] * a.shape[3], HEAD_DIM)
    cak, cav, cbk, cbv, ccv = map(frames_by_head, (cache_a_k, cache_a_v, cache_b_k, cache_b_v, cache_c_v))

    for l in range(depth):
        lam_init = 0.8 - 0.6 * math.exp(-0.3 * l)
        qa, ka, va, fa, qb, kb, vb, qc, kc, vc = _split_cols(w_in[l])
        gqc = jnp.tile(g_qc[l], 2)
        gkc = jnp.tile(g_kc[l], 2)
        q_scale = HEAD_DIM ** -0.5 * LOG2E
        prm = dict(
            w=jnp.concatenate([qa, qb, qc, ka, va, kb, vb, kc, vc], axis=1).astype(BF16),
            wf=jnp.pad(fa, ((0, 0), (0, LANES - H_A))).astype(BF16),
            bf=jnp.pad(b_f[l], (0, LANES - H_A)).reshape(1, LANES),
            gain=jnp.concatenate(
                [jnp.tile(g_qa[l] * q_scale, H_A), jnp.tile(g_qb[l] * q_scale, H_B),
                 jnp.tile(gqc * (DIFF_DIM ** -0.5 * LOG2E), H_C),
                 jnp.tile(g_ka[l], H_A), jnp.tile(ones, H_A),
                 jnp.tile(g_kb[l], H_B), jnp.tile(ones, H_B),
                 jnp.tile(gkc, H_C), jnp.tile(ones, H_C)]).reshape(1, P_WIDTH),
            g128=g128, g64=g64)
        gmix = g_mix[l].reshape(1, d)
        gmlp = g_mlp[l].reshape(1, d)
        gsub = g_subln[l].reshape(1, HEAD_DIM)
        lams = [a[l].reshape(1, DIFF_DIM) for a in (lam_q1, lam_k1, lam_q2, lam_k2)]
        wo = w_out[l].astype(BF16)
        wu = w_up[l].astype(BF16)
        wd = w_down[l].astype(BF16)
        ext = _band_bias_ext(rel_bias[l])
        ext_rev = _band_bias_reversed(ext)

        qs, kbvb, bufs_p = _project(xp, gmix, prm, cos_p, sin_p, bufs_p, tm=tm_p, layer=l, kc_rows=t)
        ka_all, va_all, kct_all, vc_all, lf_all = bufs_p
        q3 = qs.reshape(bp, t, Q_WIDTH)
        logf_rows = jnp.transpose(lf_all[l].reshape(bp, t, H_A), (0, 2, 1)).reshape(bp * H_A, t)
        c_p = _cumsum_lanes(logf_rows)
        oa = _fox_prompt(q3, ka_all.reshape(depth, bp, t, WA), va_all.reshape(depth, bp, t, WA), c_p,
                         layer=l, tq=tq)
        ob, band_p = _band_prompt(q3, kbvb.reshape(bp, t, 2 * WB), ext_rev, band_p, layer=l)
        oc = _diff_prompt(q3, kct_all, vc_all.reshape(depth, bp, t * H_C, HEAD_DIM),
                          lams, gsub, layer=l, tq=tq, lam_init=lam_init)
        xp = _merge(xp, oa.reshape(n_p, -1), ob.reshape(n_p, -1), oc.reshape(n_p, -1), wo, tm=_pick(n_p, 512))
        xp = _mlp(xp, gmlp, wu, wd, tm=_pick(n_p, 512), tf=1024)

        qs, kbvb, bufs_s = _project(xs, gmix, prm, cos_s, sin_s, bufs_s, tm=tm_s, layer=l)
        ka_all, va_all, kc_all, vc_all, lf_all = bufs_s
        q3 = qs.reshape(bs, ns, Q_WIDTH)
        la_rows = jnp.transpose(lf_all[l].reshape(bs, ns, H_A), (0, 2, 1))
        logf_all = jnp.concatenate(
            [cache_logf_rows[l], la_rows, jnp.zeros((bs, H_A, c_width - past - ns), F32)], axis=2)
        c_s = _cumsum_lanes(logf_all.reshape(bs * H_A, c_width)).reshape(bs, H_A, c_width)
        oa = _fox_sample(q3, ka_all.reshape(depth, bs, ns, WA), va_all.reshape(depth, bs, ns, WA),
                         cak, cav, c_s, layer=l)
        ob, band_s = _band_sample(q3, kbvb.reshape(bs, ns, 2 * WB), cbk, cbv, ext, band_s,
                                  layer=l, first_key_pos=past - b_rows)
        oc = _diff_sample(q3, kc_all.reshape(depth, bs, ns, WC), vc_all.reshape(depth, bs, ns * H_C, HEAD_DIM),
                          cache_c_kt, ccv, lams, gsub, layer=l, lam_init=lam_init)
        xs = _merge(xs, oa.reshape(n_s, -1), ob.reshape(n_s, -1), oc.reshape(n_s, -1), wo, tm=_pick(n_s, 512))
        xs = _mlp(xs, gmlp, wu, wd, tm=_pick(n_s, 512), tf=1024)

    pak, pav, pckt, pcv, pal = bufs_p
    sak, sav, sck, scv, sal = bufs_s
    pck = jnp.transpose(pckt.reshape(depth, bp, H_C, 2, DIFF_DIM, t), (0, 1, 5, 2, 3, 4))
    return (xp.reshape(bp, t, d), xs.reshape(bs, ns, d),
            pak.reshape(depth, bp, t, H_A, HEAD_DIM), pav.reshape(depth, bp, t, H_A, HEAD_DIM),
            pal.reshape(depth, bp, t, H_A),
            band_p[0].reshape(depth, bp, WINDOW_B, H_B, HEAD_DIM), band_p[1].reshape(depth, bp, WINDOW_B, H_B, HEAD_DIM),
            pck, pcv.reshape(depth, bp, t, H_C, HEAD_DIM),
            sak.reshape(depth, bs, ns, H_A, HEAD_DIM), sav.reshape(depth, bs, ns, H_A, HEAD_DIM),
            sal.reshape(depth, bs, ns, H_A),
            band_s[0].reshape(depth, bs, b_rows, H_B, HEAD_DIM), band_s[1].reshape(depth, bs, b_rows, H_B, HEAD_DIM),
            sck.reshape(depth, bs, ns, H_C, 2, DIFF_DIM), scv.reshape(depth, bs, ns, H_C, HEAD_DIM))
```
